```python
import math
import jax, jax.numpy as jnp
from jax import lax
import numpy as np

D_MODEL = 2048
BATCH = 4
SEQ = 2048
DEPTH = 1
DEC_BATCH = 128
DEC_SEQ = 8
PAST_LEN = 16384
PAGE_SIZE = 128

POOL_WIDTH = D_MODEL // 2
POOL_WINDOWS = (2, 4, 8, 16)
POOL_GROUPS = len(POOL_WINDOWS)
POOL_GROUP_W = POOL_WIDTH // POOL_GROUPS
POOL_BUF = max(POOL_WINDOWS) - 1
SSM_WIDTH = D_MODEL // 2
SSM_GROUP_IN = 16
SSM_GROUPS = SSM_WIDTH // SSM_GROUP_IN
SSM_STATE = 64
DT_MIN = 1e-3
DT_MAX = 1e-1
XA_HEADS = 4
XA_WIDTH = D_MODEL // 2
XA_HEAD_DIM = XA_WIDTH // XA_HEADS
N_MEM = 256
N_BRANCH = 3
IN_WIDTH = POOL_WIDTH + SSM_WIDTH + XA_WIDTH + N_BRANCH * D_MODEL
D_FF = 4 * D_MODEL
EPS = 1e-6

kernel_name = "hybrid_pool_s5_memxattn_decoder_step"

F32 = jnp.float32


def rmsnorm(x, g):
    xf = x.astype(F32)
    r = lax.rsqrt(jnp.mean(xf * xf, axis=-1, keepdims=True) + EPS)
    return (xf * r).astype(x.dtype) * g


def pool_mixer(u, prev, n_prev, pool_w, pool_scale, pool_proj):
    bsz, L, P = u.shape
    ext = jnp.concatenate([prev.astype(u.dtype), u], axis=1).astype(F32)
    cs = jnp.concatenate([jnp.zeros((bsz, 1, P), F32), jnp.cumsum(ext, axis=1)], axis=1)
    cur = ext[:, POOL_BUF:]
    end = cs[:, POOL_BUF + 1:]
    t = jnp.arange(L)
    outs = []
    for k, wdw in enumerate(POOL_WINDOWS):
        lo, hi = k * POOL_GROUP_W, (k + 1) * POOL_GROUP_W
        start = cs[:, POOL_BUF + 1 - wdw: POOL_BUF + 1 - wdw + L, lo:hi]
        cnt = jnp.minimum(t + 1 + n_prev, wdw).astype(F32)[None, :, None]
        outs.append((end[..., lo:hi] - start) / cnt - cur[..., lo:hi])
    pooled = jnp.stack(outs, axis=2)
    mixed = jnp.einsum('blgc,gcd->blgd', pooled, pool_w.astype(F32)).reshape(bsz, L, P)
    mixed = (mixed * pool_scale.astype(F32)).astype(u.dtype)
    new_buf = ext[:, -POOL_BUF:].astype(prev.dtype)
    return mixed @ pool_proj, new_buf


def _ssm_combine(e1, e2):
    a1r, a1i, b1r, b1i = e1
    a2r, a2i, b2r, b2i = e2
    ar = a2r * a1r - a2i * a1i
    ai = a2r * a1i + a2i * a1r
    br = a2r * b1r - a2i * b1i + b2r
    bi = a2r * b1i + a2i * b1r + b2i
    return (ar, ai, br, bi)


def s5_mixer(u, h_re, h_im, A_re, A_im, log_dt, B_re, B_im, C_re, C_im, D_skip, glu_w, glu_b, ssm_proj):
    bsz, L, S = u.shape
    uf = u.astype(F32)
    ug = uf.reshape(bsz, L, SSM_GROUPS, SSM_GROUP_IN)
    dt = jnp.exp(log_dt.astype(F32))[:, None]
    ar, ai = A_re.astype(F32), A_im.astype(F32)
    mag = jnp.exp(dt * ar)
    abr, abi = mag * jnp.cos(dt * ai), mag * jnp.sin(dt * ai)
    den = ar * ar + ai * ai
    xr, xi = abr - 1.0, abi
    fr = (xr * ar + xi * ai) / den
    fi = (xi * ar - xr * ai) / den
    br_, bi_ = B_re.astype(F32), B_im.astype(F32)
    bbr = fr[..., None] * br_ - fi[..., None] * bi_
    bbi = fr[..., None] * bi_ + fi[..., None] * br_
    bu_r = jnp.einsum('blgp,gnp->blgn', ug, bbr)
    bu_i = jnp.einsum('blgp,gnp->blgn', ug, bbi)
    hr0, hi0 = h_re.astype(F32), h_im.astype(F32)
    bu_r = bu_r.at[:, 0].add(abr * hr0 - abi * hi0)
    bu_i = bu_i.at[:, 0].add(abr * hi0 + abi * hr0)
    a_r = jnp.broadcast_to(abr, (1, L, SSM_GROUPS, SSM_STATE))
    a_i = jnp.broadcast_to(abi, (1, L, SSM_GROUPS, SSM_STATE))
    _, _, hr, hi = lax.associative_scan(_ssm_combine, (a_r, a_i, bu_r, bu_i), axis=1)
    y = (jnp.einsum('blgn,gpn->blgp', hr, C_re.astype(F32))
         - jnp.einsum('blgn,gpn->blgp', hi, C_im.astype(F32))).reshape(bsz, L, S)
    y = y + D_skip.astype(F32) * uf
    zg = jax.nn.gelu(y)
    zg = zg * jax.nn.sigmoid(zg @ glu_w.astype(F32) + glu_b.astype(F32))
    out = zg.astype(u.dtype) @ ssm_proj
    return out, hr[:, -1].astype(h_re.dtype), hi[:, -1].astype(h_im.dtype)


def memory_kv(mem, mem_norm_g, xa_wk, xa_wv):
    bsz, M, _ = mem.shape
    mn = rmsnorm(mem, mem_norm_g)
    k = (mn @ xa_wk).reshape(bsz, M, XA_HEADS, XA_HEAD_DIM)
    v = (mn @ xa_wv).reshape(bsz, M, XA_HEADS, XA_HEAD_DIM)
    return k, v


def memory_cross_attn(q_flat, mem_k, mem_v, xa_wo):
    bsz, L, _ = q_flat.shape
    q = q_flat.reshape(bsz, L, XA_HEADS, XA_HEAD_DIM)
    s = jnp.einsum('blhd,bmhd->bhlm', q, mem_k.astype(q.dtype)).astype(F32) * (XA_HEAD_DIM ** -0.5)
    p = jax.nn.softmax(s, axis=-1)
    o = jnp.einsum('bhlm,bmhd->blhd', p.astype(q.dtype), mem_v.astype(q.dtype)).reshape(bsz, L, XA_WIDTH)
    return o @ xa_wo


def decoder_layer(x, pool_prev, n_prev, h_re, h_im, mem_k, mem_v,
                  norm1_g, w_in, b_gate, pool_w, pool_scale, pool_proj,
                  A_re, A_im, log_dt, B_re, B_im, C_re, C_im, D_skip, glu_w, glu_b, ssm_proj,
                  xa_wo, w_out, norm2_g, mlp_w1, mlp_w2):
    bsz, L, D = x.shape
    xn = rmsnorm(x, norm1_g)
    z = xn @ w_in
    o1 = POOL_WIDTH
    o2 = o1 + SSM_WIDTH
    o3 = o2 + XA_WIDTH
    u_pool, u_ssm, q = z[..., :o1], z[..., o1:o2], z[..., o2:o3]
    gates = jax.nn.sigmoid((z[..., o3:] + b_gate.reshape(-1)).astype(F32)).reshape(bsz, L, N_BRANCH, D)
    y_pool, pool_new = pool_mixer(u_pool, pool_prev, n_prev, pool_w, pool_scale, pool_proj)
    y_ssm, hr, hi = s5_mixer(u_ssm, h_re, h_im, A_re, A_im, log_dt, B_re, B_im, C_re, C_im,
                             D_skip, glu_w, glu_b, ssm_proj)
    y_xa = memory_cross_attn(q, mem_k, mem_v, xa_wo)
    merged = (gates[:, :, 0] * y_pool.astype(F32) + gates[:, :, 1] * y_ssm.astype(F32)
              + gates[:, :, 2] * y_xa.astype(F32)).astype(x.dtype)
    x = x + merged @ w_out
    xn2 = rmsnorm(x, norm2_g)
    x = x + jnp.square(jax.nn.relu(xn2 @ mlp_w1)) @ mlp_w2
    return x, pool_new, hr, hi


def setup_inputs(seed: int = 0) -> dict:
    key = jax.random.key(seed)
    ks = iter(list(jax.random.split(key, 48)))

    def nrm(shape, scale):
        return jax.random.normal(next(ks), shape, F32) * scale

    G, N, GI = SSM_GROUPS, SSM_STATE, SSM_GROUP_IN
    n_idx = jnp.arange(N, dtype=F32)[None, None, :]
    return {
        "x_prompt": nrm((BATCH, SEQ, D_MODEL), 1.0),
        "x_sample": nrm((DEC_BATCH, DEC_SEQ, D_MODEL), 1.0),
        "state_pool": nrm((DEPTH, DEC_BATCH, POOL_BUF, POOL_WIDTH), 1.0),
        "state_ssm_re": nrm((DEPTH, DEC_BATCH, G, N), 1.0),
        "state_ssm_im": nrm((DEPTH, DEC_BATCH, G, N), 1.0),
        "cache_mem_k": nrm((DEPTH, DEC_BATCH, N_MEM, XA_HEADS, XA_HEAD_DIM), 1.0),
        "cache_mem_v": nrm((DEPTH, DEC_BATCH, N_MEM, XA_HEADS, XA_HEAD_DIM), 1.0),
        "mem_prompt": nrm((BATCH, N_MEM, D_MODEL), 1.0),
        "norm1_g": 1.0 + nrm((DEPTH, D_MODEL), 0.02),
        "w_in": nrm((DEPTH, D_MODEL, IN_WIDTH), D_MODEL ** -0.5),
        "b_gate": nrm((DEPTH, N_BRANCH, D_MODEL), 0.01),
        "pool_w": nrm((DEPTH, POOL_GROUPS, POOL_GROUP_W, POOL_GROUP_W), POOL_GROUP_W ** -0.5),
        "pool_scale": 1.0 + nrm((DEPTH, POOL_WIDTH), 0.1),
        "pool_proj": nrm((DEPTH, POOL_WIDTH, D_MODEL), POOL_WIDTH ** -0.5),
        "ssm_A_re": -0.5 + nrm((DEPTH, G, N), 0.01),
        "ssm_A_im": jnp.pi * n_idx + nrm((DEPTH, G, N), 0.01),
        "ssm_log_dt": jax.random.uniform(next(ks), (DEPTH, G), F32, math.log(DT_MIN), math.log(DT_MAX)),
        "ssm_B_re": nrm((DEPTH, G, N, GI), (2 * GI) ** -0.5),
        "ssm_B_im": nrm((DEPTH, G, N, GI), (2 * GI) ** -0.5),
        "ssm_C_re": nrm((DEPTH, G, GI, N), (2 * N) ** -0.5),
        "ssm_C_im": nrm((DEPTH, G, GI, N), (2 * N) ** -0.5),
        "ssm_D": nrm((DEPTH, SSM_WIDTH), 1.0),
        "ssm_glu_w": nrm((DEPTH, SSM_WIDTH, SSM_WIDTH), SSM_WIDTH ** -0.5),
        "ssm_glu_b": nrm((DEPTH, SSM_WIDTH), 0.01),
        "ssm_proj": nrm((DEPTH, SSM_WIDTH, D_MODEL), SSM_WIDTH ** -0.5),
        "mem_norm_g": 1.0 + nrm((DEPTH, D_MODEL), 0.02),
        "xa_wk": nrm((DEPTH, D_MODEL, XA_WIDTH), D_MODEL ** -0.5),
        "xa_wv": nrm((DEPTH, D_MODEL, XA_WIDTH), D_MODEL ** -0.5),
        "xa_wo": nrm((DEPTH, XA_WIDTH, D_MODEL), XA_WIDTH ** -0.5),
        "w_out": nrm((DEPTH, D_MODEL, D_MODEL), D_MODEL ** -0.5),
        "norm2_g": 1.0 + nrm((DEPTH, D_MODEL), 0.02),
        "mlp_w1": nrm((DEPTH, D_MODEL, D_FF), D_MODEL ** -0.5),
        "mlp_w2": nrm((DEPTH, D_FF, D_MODEL), D_FF ** -0.5),
        "final_norm_g": 1.0 + nrm((D_MODEL,), 0.02),
    }


def reference(x_prompt, x_sample, state_pool, state_ssm_re, state_ssm_im, cache_mem_k, cache_mem_v,
              mem_prompt, norm1_g, w_in, b_gate, pool_w, pool_scale, pool_proj,
              ssm_A_re, ssm_A_im, ssm_log_dt, ssm_B_re, ssm_B_im, ssm_C_re, ssm_C_im, ssm_D,
              ssm_glu_w, ssm_glu_b, ssm_proj, mem_norm_g, xa_wk, xa_wv, xa_wo, w_out,
              norm2_g, mlp_w1, mlp_w2, final_norm_g):
    n_prev_sample = min(POOL_BUF, PAST_LEN)
    xp, xs = x_prompt, x_sample
    pool_p, re_p, im_p, mk_p, mv_p = [], [], [], [], []
    pool_s, re_s, im_s = [], [], []
    for l in range(DEPTH):
        lw = (norm1_g[l], w_in[l], b_gate[l], pool_w[l], pool_scale[l], pool_proj[l],
              ssm_A_re[l], ssm_A_im[l], ssm_log_dt[l], ssm_B_re[l], ssm_B_im[l], ssm_C_re[l], ssm_C_im[l],
              ssm_D[l], ssm_glu_w[l], ssm_glu_b[l], ssm_proj[l], xa_wo[l], w_out[l],
              norm2_g[l], mlp_w1[l], mlp_w2[l])
        mk, mv = memory_kv(mem_prompt, mem_norm_g[l], xa_wk[l], xa_wv[l])
        zero_pool = jnp.zeros((xp.shape[0], POOL_BUF, POOL_WIDTH), state_pool.dtype)
        zero_h = jnp.zeros((xp.shape[0], SSM_GROUPS, SSM_STATE), state_ssm_re.dtype)
        xp, pb, hr, hi = decoder_layer(xp, zero_pool, 0, zero_h, zero_h, mk, mv, *lw)
        pool_p.append(pb); re_p.append(hr); im_p.append(hi); mk_p.append(mk); mv_p.append(mv)
        xs, pb, hr, hi = decoder_layer(xs, state_pool[l], n_prev_sample, state_ssm_re[l], state_ssm_im[l],
                                       cache_mem_k[l], cache_mem_v[l], *lw)
        pool_s.append(pb); re_s.append(hr); im_s.append(hi)
    y_prompt = rmsnorm(xp, final_norm_g)
    y_sample = rmsnorm(xs, final_norm_g)
    return (y_prompt, y_sample,
            jnp.stack(pool_p), jnp.stack(re_p), jnp.stack(im_p), jnp.stack(mk_p), jnp.stack(mv_p),
            jnp.stack(pool_s), jnp.stack(re_s), jnp.stack(im_s))
```

```python
import functools
import math

import jax
import jax.numpy as jnp
from jax import lax
from jax.experimental import pallas as pl
from jax.experimental.pallas import tpu as pltpu

F32 = jnp.float32
BF16 = jnp.bfloat16

D_MODEL = 2048
POOL_WINDOWS = (2, 4, 8, 16)
POOL_GROUP_W = 256
POOL_WIDTH = 1024
POOL_BUF = 15
SSM_WIDTH = 1024
SSM_GROUPS = 64
SSM_GROUP_IN = 16
SSM_STATE = 64
SSM_LANES = SSM_GROUPS * SSM_STATE
SSM_CHUNKS = 4
SSM_CHUNK_GROUPS = SSM_GROUPS // SSM_CHUNKS
SSM_CHUNK_IN = SSM_WIDTH // SSM_CHUNKS
SSM_CHUNK_LANES = SSM_LANES // SSM_CHUNKS
XA_HEADS = 4
XA_HEAD_DIM = 256
XA_WIDTH = 1024
N_MEM = 256
N_BRANCH = 3
D_FF = 4 * D_MODEL
EPS = 1e-6
PAST_LEN = 16384

SUBLANES = 8
POOL_HIST = 2 * SUBLANES
SCAN_STEPS = (1, 2, 4)
SCAN_LANE_BLOCK = 512
VMEM_LIMIT = 56 * 1024 * 1024


def _params(*sem):
    return pltpu.CompilerParams(dimension_semantics=sem, vmem_limit_bytes=VMEM_LIMIT)


def _resident(shape):
    nd = len(shape)
    return pl.BlockSpec(shape, lambda *_: (0,) * nd, pipeline_mode=pl.Buffered(1))


def _rms_scale(x):
    return lax.rsqrt(jnp.mean(x * x, axis=-1, keepdims=True) + EPS)


def _dot(a, b):
    return jnp.dot(a, b, preferred_element_type=F32)


def _norm_matmul_kernel(*refs, gate_bias):
    if gate_bias:
        x_ref, g_ref, w_ref, b_ref, o_ref, xn_ref = refs
    else:
        x_ref, g_ref, w_ref, o_ref, xn_ref = refs

    @pl.when(pl.program_id(1) == 0)
    def _():
        x = x_ref[...]
        xn_ref[...] = ((x * _rms_scale(x)) * g_ref[...]).astype(BF16)

    acc = _dot(xn_ref[...], w_ref[...])
    if gate_bias:
        acc = jax.nn.sigmoid(acc + b_ref[...])
    o_ref[...] = acc


def _norm_matmul(x, g, w, bias=None, *, tm, tn, name):
    m, d = x.shape
    n = w.shape[1]
    in_specs = [pl.BlockSpec((tm, d), lambda i, j: (i, 0)),
                pl.BlockSpec((1, d), lambda i, j: (0, 0)),
                pl.BlockSpec((d, tn), lambda i, j: (0, j))]
    args = [x, g.reshape(1, d), w]
    if bias is not None:
        in_specs.append(pl.BlockSpec((1, tn), lambda i, j: (0, j)))
        args.append(bias.reshape(1, n))
    return pl.pallas_call(
        functools.partial(_norm_matmul_kernel, gate_bias=bias is not None),
        grid=(m // tm, n // tn),
        in_specs=in_specs,
        out_specs=pl.BlockSpec((tm, tn), lambda i, j: (i, j)),
        out_shape=jax.ShapeDtypeStruct((m, n), F32),
        scratch_shapes=[pltpu.VMEM((tm, d), BF16)],
        compiler_params=_params("parallel", "arbitrary"),
        name=name,
    )(*args)


def _window_means(e_ref, row0, rows, first_pos):
    outs = []
    for k, wdw in enumerate(POOL_WINDOWS):
        lo = k * POOL_GROUP_W
        cur = e_ref[row0:row0 + rows, lo:lo + POOL_GROUP_W]
        s = cur
        for d in range(1, wdw):
            s = s + e_ref[row0 - d:row0 - d + rows, lo:lo + POOL_GROUP_W]
        pos = lax.broadcasted_iota(jnp.int32, (rows, POOL_GROUP_W), 0) + first_pos
        cnt = jnp.minimum(pos, wdw).astype(F32)
        outs.append(s / cnt - cur)
    return outs


def _pool_mix(pooled, k, pw_ref, ps_ref):
    lo = k * POOL_GROUP_W
    mixed = _dot(pooled.astype(BF16), pw_ref[k])
    return (mixed * ps_ref[:, lo:lo + POOL_GROUP_W]).astype(BF16)


def _pool_prompt_kernel(u_ref, pw_ref, ps_ref, o_ref, st_ref, e_ref, *, rows):
    l = pl.program_id(1)

    @pl.when(l == 0)
    def _():
        e_ref[0:POOL_HIST, :] = jnp.zeros((POOL_HIST, POOL_WIDTH), F32)

    @pl.when(l > 0)
    def _():
        e_ref[0:POOL_HIST, :] = e_ref[rows:rows + POOL_HIST, :]

    e_ref[POOL_HIST:POOL_HIST + rows, :] = u_ref[...]
    pooled = _window_means(e_ref, POOL_HIST, rows, l * rows + 1)
    for k, p in enumerate(pooled):
        o_ref[:, k * POOL_GROUP_W:(k + 1) * POOL_GROUP_W] = _pool_mix(p, k, pw_ref, ps_ref)

    @pl.when(l == pl.num_programs(1) - 1)
    def _():
        st_ref[0] = e_ref[POOL_HIST + rows - POOL_BUF:POOL_HIST + rows, :]


def _pool_prompt(z, pool_w, pool_scale, *, batch, seq, rows):
    nl = seq // rows
    return pl.pallas_call(
        functools.partial(_pool_prompt_kernel, rows=rows),
        grid=(batch, nl),
        in_specs=[pl.BlockSpec((rows, POOL_WIDTH), lambda b, l: (b * nl + l, 0)),
                  _resident(pool_w.shape), _resident(pool_scale.shape)],
        out_specs=[pl.BlockSpec((rows, POOL_WIDTH), lambda b, l: (b * nl + l, 0)),
                   pl.BlockSpec((1, POOL_BUF, POOL_WIDTH), lambda b, l: (b, 0, 0))],
        out_shape=[jax.ShapeDtypeStruct((batch * seq, POOL_WIDTH), BF16),
                   jax.ShapeDtypeStruct((batch, POOL_BUF, POOL_WIDTH), F32)],
        scratch_shapes=[pltpu.VMEM((rows + POOL_HIST, POOL_WIDTH), F32)],
        compiler_params=_params("parallel", "arbitrary"),
        name="pool_prompt",
    )(z, pool_w, pool_scale)


def _pool_sample_kernel(u_ref, prev_ref, pw_ref, ps_ref, o_ref, st_ref, e_ref, p_ref, *, bb, seq, n_prev):
    span = POOL_HIST + seq
    for b in range(bb):
        top = b * span + POOL_HIST
        e_ref[top - POOL_BUF:top, :] = prev_ref[b]
        e_ref[top:top + seq, :] = u_ref[b * seq:(b + 1) * seq, :]
    for b in range(bb):
        top = b * span + POOL_HIST
        st_ref[b] = e_ref[top + seq - POOL_BUF:top + seq, :]
        pooled = _window_means(e_ref, top, seq, n_prev + 1)
        for k, p in enumerate(pooled):
            p_ref[b * seq:(b + 1) * seq, k * POOL_GROUP_W:(k + 1) * POOL_GROUP_W] = p
    for k in range(len(POOL_WINDOWS)):
        lo = k * POOL_GROUP_W
        o_ref[:, lo:lo + POOL_GROUP_W] = _pool_mix(p_ref[:, lo:lo + POOL_GROUP_W], k, pw_ref, ps_ref)


def _pool_sample(z, prev, pool_w, pool_scale, *, batch, seq, bb, n_prev):
    return pl.pallas_call(
        functools.partial(_pool_sample_kernel, bb=bb, seq=seq, n_prev=n_prev),
        grid=(batch // bb,),
        in_specs=[pl.BlockSpec((bb * seq, POOL_WIDTH), lambda i: (i, 0)),
                  pl.BlockSpec((bb, POOL_BUF, POOL_WIDTH), lambda i: (i, 0, 0)),
                  _resident(pool_w.shape), _resident(pool_scale.shape)],
        out_specs=[pl.BlockSpec((bb * seq, POOL_WIDTH), lambda i: (i, 0)),
                   pl.BlockSpec((bb, POOL_BUF, POOL_WIDTH), lambda i: (i, 0, 0))],
        out_shape=[jax.ShapeDtypeStruct((batch * seq, POOL_WIDTH), BF16),
                   jax.ShapeDtypeStruct((batch, POOL_BUF, POOL_WIDTH), F32)],
        scratch_shapes=[pltpu.VMEM((bb * (POOL_HIST + seq), POOL_WIDTH), F32),
                        pltpu.VMEM((bb * seq, POOL_WIDTH), F32)],
        compiler_params=_params("parallel"),
        name="pool_sample",
    )(z, prev, pool_w, pool_scale)


def _ssm_input_map(u_ref, wbr_ref, wbi_ref, hr_ref, hi_ref):
    for c in range(SSM_CHUNKS):
        uc = u_ref[:, c * SSM_CHUNK_IN:(c + 1) * SSM_CHUNK_IN].astype(BF16)
        lanes = slice(c * SSM_CHUNK_LANES, (c + 1) * SSM_CHUNK_LANES)
        hr_ref[:, lanes] = _dot(uc, wbr_ref[c])
        hi_ref[:, lanes] = _dot(uc, wbi_ref[c])


def _scan_rows(xr, xi, coef, car_r, car_i):
    for n, s in enumerate(SCAN_STEPS):
        cr, ci = coef[2 * n], coef[2 * n + 1]
        rr = pltpu.roll(xr, s, 0)
        ri = pltpu.roll(xi, s, 0)
        xr, xi = xr + (cr * rr - ci * ri), xi + (cr * ri + ci * rr)
    pr, pi = coef[2 * len(SCAN_STEPS)], coef[2 * len(SCAN_STEPS) + 1]
    xr = xr + (pr * car_r - pi * car_i)
    xi = xi + (pr * car_i + pi * car_r)
    return xr, xi


def _last_row(x):
    return jnp.broadcast_to(x[SUBLANES - 1:SUBLANES, :], x.shape)


def _ssm_output(u_ref, hr_ref, hi_ref, wcr_ref, wci_ref, d_ref, gw_ref, gb_ref, zg_ref, o_ref):
    for c in range(SSM_CHUNKS):
        lanes = slice(c * SSM_CHUNK_LANES, (c + 1) * SSM_CHUNK_LANES)
        cols = slice(c * SSM_CHUNK_IN, (c + 1) * SSM_CHUNK_IN)
        y = (_dot(hr_ref[:, lanes].astype(BF16), wcr_ref[c])
             - _dot(hi_ref[:, lanes].astype(BF16), wci_ref[c]))
        y = y + d_ref[:, cols] * u_ref[:, cols]
        zg_ref[:, cols] = jax.nn.gelu(y)
    zg = zg_ref[...]
    gate = jax.nn.sigmoid(_dot(zg.astype(BF16), gw_ref[...]) + gb_ref[...])
    o_ref[...] = (zg * gate).astype(BF16)


def _ssm_prompt_kernel(u_ref, wbr_ref, wbi_ref, wcr_ref, wci_ref, coef_ref, d_ref, gw_ref, gb_ref,
                       o_ref, sr_ref, si_ref, hr_ref, hi_ref, zg_ref, car_ref, *, rows):
    l = pl.program_id(1)

    @pl.when(l == 0)
    def _():
        car_ref[...] = jnp.zeros(car_ref.shape, F32)

    _ssm_input_map(u_ref, wbr_ref, wbi_ref, hr_ref, hi_ref)

    for lb in range(SSM_LANES // SCAN_LANE_BLOCK):
        lanes = slice(lb * SCAN_LANE_BLOCK, (lb + 1) * SCAN_LANE_BLOCK)
        coef = tuple(coef_ref[a, :, lanes] for a in range(2 * len(SCAN_STEPS) + 2))

        def body(k, car):
            r0 = pl.multiple_of(k * SUBLANES, SUBLANES)
            xr, xi = _scan_rows(hr_ref[pl.ds(r0, SUBLANES), lanes], hi_ref[pl.ds(r0, SUBLANES), lanes],
                                coef, car[0], car[1])
            hr_ref[pl.ds(r0, SUBLANES), lanes] = xr
            hi_ref[pl.ds(r0, SUBLANES), lanes] = xi
            return _last_row(xr), _last_row(xi)

        car = lax.fori_loop(0, rows // SUBLANES, body, (car_ref[0, :, lanes], car_ref[1, :, lanes]), unroll=2)
        car_ref[0, :, lanes] = car[0]
        car_ref[1, :, lanes] = car[1]

    _ssm_output(u_ref, hr_ref, hi_ref, wcr_ref, wci_ref, d_ref, gw_ref, gb_ref, zg_ref, o_ref)

    @pl.when(l == pl.num_programs(1) - 1)
    def _():
        sr_ref[0] = hr_ref[rows - 1:rows, :]
        si_ref[0] = hi_ref[rows - 1:rows, :]


def _ssm_weight_specs(w):
    return [_resident(w["wb_re"].shape), _resident(w["wb_im"].shape),
            _resident(w["wc_re"].shape), _resident(w["wc_im"].shape),
            _resident(w["coef"].shape), _resident(w["d"].shape),
            _resident(w["glu_w"].shape), _resident(w["glu_b"].shape)]


def _ssm_weight_args(w):
    return [w["wb_re"], w["wb_im"], w["wc_re"], w["wc_im"], w["coef"], w["d"], w["glu_w"], w["glu_b"]]


def _ssm_prompt(z, w, *, batch, seq, rows):
    nl = seq // rows
    return pl.pallas_call(
        functools.partial(_ssm_prompt_kernel, rows=rows),
        grid=(batch, nl),
        in_specs=[pl.BlockSpec((rows, SSM_WIDTH), lambda b, l: (b * nl + l, 1))] + _ssm_weight_specs(w),
        out_specs=[pl.BlockSpec((rows, SSM_WIDTH), lambda b, l: (b * nl + l, 0)),
                   pl.BlockSpec((1, 1, SSM_LANES), lambda b, l: (b, 0, 0)),
                   pl.BlockSpec((1, 1, SSM_LANES), lambda b, l: (b, 0, 0))],
        out_shape=[jax.ShapeDtypeStruct((batch * seq, SSM_WIDTH), BF16),
                   jax.ShapeDtypeStruct((batch, 1, SSM_LANES), F32),
                   jax.ShapeDtypeStruct((batch, 1, SSM_LANES), F32)],
        scratch_shapes=[pltpu.VMEM((rows, SSM_LANES), F32), pltpu.VMEM((rows, SSM_LANES), F32),
                        pltpu.VMEM((rows, SSM_WIDTH), F32),
                        pltpu.VMEM((2, SUBLANES, SSM_LANES), F32)],
        compiler_params=_params("parallel", "arbitrary"),
        name="ssm_prompt",
    )(z, *_ssm_weight_args(w))


def _ssm_sample_kernel(u_ref, h0r_ref, h0i_ref, wbr_ref, wbi_ref, wcr_ref, wci_ref, coef_ref, d_ref,
                       gw_ref, gb_ref, o_ref, sr_ref, si_ref, hr_ref, hi_ref, zg_ref, *, bb):
    _ssm_input_map(u_ref, wbr_ref, wbi_ref, hr_ref, hi_ref)

    for lb in range(SSM_LANES // SCAN_LANE_BLOCK):
        lanes = slice(lb * SCAN_LANE_BLOCK, (lb + 1) * SCAN_LANE_BLOCK)
        coef = tuple(coef_ref[a, :, lanes] for a in range(2 * len(SCAN_STEPS) + 2))

        def body(k, carry):
            r0 = pl.multiple_of(k * SUBLANES, SUBLANES)
            car_r = jnp.broadcast_to(h0r_ref[pl.ds(k, 1), lanes], (SUBLANES, SCAN_LANE_BLOCK))
            car_i = jnp.broadcast_to(h0i_ref[pl.ds(k, 1), lanes], (SUBLANES, SCAN_LANE_BLOCK))
            xr, xi = _scan_rows(hr_ref[pl.ds(r0, SUBLANES), lanes], hi_ref[pl.ds(r0, SUBLANES), lanes],
                                coef, car_r, car_i)
            hr_ref[pl.ds(r0, SUBLANES), lanes] = xr
            hi_ref[pl.ds(r0, SUBLANES), lanes] = xi
            sr_ref[pl.ds(k, 1), lanes] = xr[SUBLANES - 1:SUBLANES, :]
            si_ref[pl.ds(k, 1), lanes] = xi[SUBLANES - 1:SUBLANES, :]
            return carry

        lax.fori_loop(0, bb, body, 0, unroll=2)

    _ssm_output(u_ref, hr_ref, hi_ref, wcr_ref, wci_ref, d_ref, gw_ref, gb_ref, zg_ref, o_ref)


def _ssm_sample(z, h_re, h_im, w, *, batch, seq, bb):
    assert seq == SUBLANES
    rows = bb * seq
    return pl.pallas_call(
        functools.partial(_ssm_sample_kernel, bb=bb),
        grid=(batch // bb,),
        in_specs=[pl.BlockSpec((rows, SSM_WIDTH), lambda i: (i, 1)),
                  pl.BlockSpec((bb, SSM_LANES), lambda i: (i, 0)),
                  pl.BlockSpec((bb, SSM_LANES), lambda i: (i, 0))] + _ssm_weight_specs(w),
        out_specs=[pl.BlockSpec((rows, SSM_WIDTH), lambda i: (i, 0)),
                   pl.BlockSpec((bb, SSM_LANES), lambda i: (i, 0)),
                   pl.BlockSpec((bb, SSM_LANES), lambda i: (i, 0))],
        out_shape=[jax.ShapeDtypeStruct((batch * seq, SSM_WIDTH), BF16),
                   jax.ShapeDtypeStruct((batch, SSM_LANES), F32),
                   jax.ShapeDtypeStruct((batch, SSM_LANES), F32)],
        scratch_shapes=[pltpu.VMEM((rows, SSM_LANES), F32), pltpu.VMEM((rows, SSM_LANES), F32),
                        pltpu.VMEM((rows, SSM_WIDTH), F32)],
        compiler_params=_params("parallel"),
        name="ssm_sample",
    )(z, h_re, h_im, *_ssm_weight_args(w))


def _ssm_weights(a_re, a_im, log_dt, b_re, b_im, c_re, c_im, d_skip, glu_w, glu_b):
    dt = jnp.exp(log_dt)[:, None]
    mag = jnp.exp(dt * a_re)
    abr, abi = mag * jnp.cos(dt * a_im), mag * jnp.sin(dt * a_im)
    den = a_re * a_re + a_im * a_im
    xr, xi = abr - 1.0, abi
    fr = (xr * a_re + xi * a_im) / den
    fi = (xi * a_re - xr * a_im) / den
    bbr = fr[..., None] * b_re - fi[..., None] * b_im
    bbi = fr[..., None] * b_im + fi[..., None] * b_re
    eye = jnp.eye(SSM_CHUNK_GROUPS, dtype=F32)

    def in_map(bb):
        t = bb.transpose(0, 2, 1).reshape(SSM_CHUNKS, SSM_CHUNK_GROUPS, SSM_GROUP_IN, SSM_STATE)
        t = t[:, :, :, None, :] * eye[None, :, None, :, None]
        return t.reshape(SSM_CHUNKS, SSM_CHUNK_IN, SSM_CHUNK_LANES).astype(BF16)

    def out_map(cc):
        t = cc.reshape(SSM_CHUNKS, SSM_CHUNK_GROUPS, SSM_GROUP_IN, SSM_STATE).transpose(0, 1, 3, 2)
        t = t[:, :, :, None, :] * eye[None, :, None, :, None]
        return t.reshape(SSM_CHUNKS, SSM_CHUNK_LANES, SSM_CHUNK_IN).astype(BF16)

    ar, ai = abr.reshape(1, SSM_LANES), abi.reshape(1, SSM_LANES)
    pows = [(ar, ai)]
    for _ in range(SUBLANES - 1):
        pr, pi = pows[-1]
        pows.append((pr * ar - pi * ai, pr * ai + pi * ar))
    row = jnp.arange(SUBLANES)[:, None]
    tables = []
    for s in SCAN_STEPS:
        pr, pi = pows[s - 1]
        tables += [jnp.where(row >= s, pr, 0.0), jnp.where(row >= s, pi, 0.0)]
    tables += [jnp.concatenate([p[0] for p in pows], axis=0), jnp.concatenate([p[1] for p in pows], axis=0)]
    return {"wb_re": in_map(bbr), "wb_im": in_map(bbi), "wc_re": out_map(c_re), "wc_im": out_map(c_im),
            "coef": jnp.stack(tables), "d": d_skip.reshape(1, SSM_WIDTH),
            "glu_w": glu_w.astype(BF16), "glu_b": glu_b.reshape(1, SSM_WIDTH)}


def _attend(q, k, v):
    s = lax.dot_general(q, k, (((1,), (1,)), ((), ())), preferred_element_type=F32)
    s = s * (XA_HEAD_DIM ** -0.5)
    p = jnp.exp(s - jnp.max(s, axis=-1, keepdims=True))
    p = p / jnp.sum(p, axis=-1, keepdims=True)
    return _dot(p.astype(BF16), v)


def _xa_prompt_kernel(q_ref, k_ref, v_ref, o_ref):
    for h in range(XA_HEADS):
        cols = slice(h * XA_HEAD_DIM, (h + 1) * XA_HEAD_DIM)
        o = _attend(q_ref[:, cols].astype(BF16), k_ref[:, cols].astype(BF16), v_ref[:, cols].astype(BF16))
        o_ref[:, cols] = o.astype(BF16)


def _xa_prompt(z, kv, *, batch, seq, rows):
    nl = seq // rows
    return pl.pallas_call(
        _xa_prompt_kernel,
        grid=(batch, nl),
        in_specs=[pl.BlockSpec((rows, XA_WIDTH), lambda b, l: (b * nl + l, 2)),
                  pl.BlockSpec((N_MEM, XA_WIDTH), lambda b, l: (b, 0)),
                  pl.BlockSpec((N_MEM, XA_WIDTH), lambda b, l: (b, 1))],
        out_specs=pl.BlockSpec((rows, XA_WIDTH), lambda b, l: (b * nl + l, 0)),
        out_shape=jax.ShapeDtypeStruct((batch * seq, XA_WIDTH), BF16),
        compiler_params=_params("parallel", "arbitrary"),
        name="xattn_prompt",
    )(z, kv, kv)


def _xa_sample_kernel(q_ref, k_ref, v_ref, o_ref, *, bb, seq):
    for b in range(bb):
        rows = slice(b * seq, (b + 1) * seq)
        for h in range(XA_HEADS):
            cols = slice(h * XA_HEAD_DIM, (h + 1) * XA_HEAD_DIM)
            o = _attend(q_ref[rows, cols].astype(BF16), k_ref[b, :, cols].astype(BF16),
                        v_ref[b, :, cols].astype(BF16))
            o_ref[rows, cols] = o.astype(BF16)


def _xa_sample(z, mem_k, mem_v, *, batch, seq, bb):
    return pl.pallas_call(
        functools.partial(_xa_sample_kernel, bb=bb, seq=seq),
        grid=(batch // bb,),
        in_specs=[pl.BlockSpec((bb * seq, XA_WIDTH), lambda i: (i, 2)),
                  pl.BlockSpec((bb, N_MEM, XA_WIDTH), lambda i: (i, 0, 0)),
                  pl.BlockSpec((bb, N_MEM, XA_WIDTH), lambda i: (i, 0, 0))],
        out_specs=pl.BlockSpec((bb * seq, XA_WIDTH), lambda i: (i, 0)),
        out_shape=jax.ShapeDtypeStruct((batch * seq, XA_WIDTH), BF16),
        compiler_params=_params("parallel"),
        name="xattn_sample",
    )(z, mem_k, mem_v)


def _merge_kernel(ap_ref, as_ref, ax_ref, g0_ref, g1_ref, g2_ref, x_ref, wp_ref, ws_ref, wx_ref, wo_ref, o_ref):
    merged = (g0_ref[...] * _dot(ap_ref[...], wp_ref[...])
              + g1_ref[...] * _dot(as_ref[...], ws_ref[...])
              + g2_ref[...] * _dot(ax_ref[...], wx_ref[...]))
    o_ref[...] = x_ref[...] + _dot(merged.astype(BF16), wo_ref[...])


def _merge(a_pool, a_ssm, a_xa, gates, x, wp, ws, wx, wo, *, tm):
    m, d = x.shape
    act = pl.BlockSpec((tm, a_pool.shape[1]), lambda i: (i, 0))
    return pl.pallas_call(
        _merge_kernel,
        grid=(m // tm,),
        in_specs=[act, act, act,
                  pl.BlockSpec((tm, d), lambda i: (i, 0)),
                  pl.BlockSpec((tm, d), lambda i: (i, 1)),
                  pl.BlockSpec((tm, d), lambda i: (i, 2)),
                  pl.BlockSpec((tm, d), lambda i: (i, 0)),
                  _resident(wp.shape), _resident(ws.shape), _resident(wx.shape), _resident(wo.shape)],
        out_specs=pl.BlockSpec((tm, d), lambda i: (i, 0)),
        out_shape=jax.ShapeDtypeStruct((m, d), F32),
        compiler_params=_params("parallel"),
        name="merge",
    )(a_pool, a_ssm, a_xa, gates, gates, gates, x, wp, ws, wx, wo)


def _mlp_kernel(x_ref, g_ref, w1_ref, w2_ref, gf_ref, o_ref, xn_ref, acc_ref):
    f = pl.program_id(1)

    @pl.when(f == 0)
    def _():
        x = x_ref[...]
        xn_ref[...] = ((x * _rms_scale(x)) * g_ref[...]).astype(BF16)
        acc_ref[...] = x

    h = jnp.square(jnp.maximum(_dot(xn_ref[...], w1_ref[...]), 0.0))
    acc_ref[...] += _dot(h.astype(BF16), w2_ref[...])

    @pl.when(f == pl.num_programs(1) - 1)
    def _():
        y = acc_ref[...]
        o_ref[...] = (y * _rms_scale(y)) * gf_ref[...]


def _mlp(x, g, w1, w2, gf, *, tm, tf):
    m, d = x.shape
    dff = w1.shape[1]
    return pl.pallas_call(
        _mlp_kernel,
        grid=(m // tm, dff // tf),
        in_specs=[pl.BlockSpec((tm, d), lambda i, f: (i, 0)),
                  pl.BlockSpec((1, d), lambda i, f: (0, 0)),
                  pl.BlockSpec((d, tf), lambda i, f: (0, f)),
                  pl.BlockSpec((tf, d), lambda i, f: (f, 0)),
                  pl.BlockSpec((1, d), lambda i, f: (0, 0))],
        out_specs=pl.BlockSpec((tm, d), lambda i, f: (i, 0)),
        out_shape=jax.ShapeDtypeStruct((m, d), F32),
        scratch_shapes=[pltpu.VMEM((tm, d), BF16), pltpu.VMEM((tm, d), F32)],
        compiler_params=_params("parallel", "arbitrary"),
        name="mlp",
    )(x, g.reshape(1, d), w1, w2, gf.reshape(1, d))


def _layer_weights(l, norm1_g, w_in, b_gate, pool_w, pool_scale, pool_proj, ssm_A_re, ssm_A_im, ssm_log_dt,
                   ssm_B_re, ssm_B_im, ssm_C_re, ssm_C_im, ssm_D, ssm_glu_w, ssm_glu_b, ssm_proj,
                   mem_norm_g, xa_wk, xa_wv, xa_wo, w_out, norm2_g, mlp_w1, mlp_w2):
    n_mix = POOL_WIDTH + SSM_WIDTH + XA_WIDTH
    w_in_l = w_in[l].astype(BF16)
    return {
        "norm1_g": norm1_g[l], "w_mix": w_in_l[:, :n_mix], "w_gate": w_in_l[:, n_mix:],
        "b_gate": b_gate[l].reshape(-1),
        "pool_w": pool_w[l].astype(BF16), "pool_scale": pool_scale[l].reshape(1, POOL_WIDTH),
        "pool_proj": pool_proj[l].astype(BF16),
        "ssm": _ssm_weights(ssm_A_re[l], ssm_A_im[l], ssm_log_dt[l], ssm_B_re[l], ssm_B_im[l],
                            ssm_C_re[l], ssm_C_im[l], ssm_D[l], ssm_glu_w[l], ssm_glu_b[l]),
        "ssm_proj": ssm_proj[l].astype(BF16),
        "mem_norm_g": mem_norm_g[l],
        "w_kv": jnp.concatenate([xa_wk[l], xa_wv[l]], axis=1).astype(BF16),
        "xa_wo": xa_wo[l].astype(BF16), "w_out": w_out[l].astype(BF16),
        "norm2_g": norm2_g[l], "mlp_w1": mlp_w1[l].astype(BF16), "mlp_w2": mlp_w2[l].astype(BF16),
    }


def _token_tile(m, cap):
    t = min(m, cap)
    assert m % t == 0
    return t


def _finish_layer(x, z, gates, a_pool, a_ssm, a_xa, w, final_g):
    m = x.shape[0]
    x = _merge(a_pool, a_ssm, a_xa, gates, x, w["pool_proj"], w["ssm_proj"], w["xa_wo"], w["w_out"],
               tm=_token_tile(m, 256))
    return _mlp(x, w["norm2_g"], w["mlp_w1"], w["mlp_w2"], final_g, tm=_token_tile(m, 512), tf=1024)


def kernel(x_prompt, x_sample, state_pool, state_ssm_re, state_ssm_im, cache_mem_k, cache_mem_v, mem_prompt,
           norm1_g, w_in, b_gate, pool_w, pool_scale, pool_proj, ssm_A_re, ssm_A_im, ssm_log_dt, ssm_B_re,
           ssm_B_im, ssm_C_re, ssm_C_im, ssm_D, ssm_glu_w, ssm_glu_b, ssm_proj, mem_norm_g, xa_wk, xa_wv,
           xa_wo, w_out, norm2_g, mlp_w1, mlp_w2, final_norm_g):
    depth = norm1_g.shape[0]
    assert depth == 1, "the final norm is fused into the single layer's MLP kernel"
    bp, lp, d = x_prompt.shape
    bs, ls, _ = x_sample.shape
    n_prev_sample = min(POOL_BUF, PAST_LEN)
    w = _layer_weights(0, norm1_g, w_in, b_gate, pool_w, pool_scale, pool_proj, ssm_A_re, ssm_A_im, ssm_log_dt,
                       ssm_B_re, ssm_B_im, ssm_C_re, ssm_C_im, ssm_D, ssm_glu_w, ssm_glu_b, ssm_proj,
                       mem_norm_g, xa_wk, xa_wv, xa_wo, w_out, norm2_g, mlp_w1, mlp_w2)

    xp = x_prompt.reshape(bp * lp, d)
    kv = _norm_matmul(mem_prompt.reshape(bp * N_MEM, d), w["mem_norm_g"], w["w_kv"],
                      tm=_token_tile(bp * N_MEM, 512), tn=1024, name="memory_kv")
    zp = _norm_matmul(xp, w["norm1_g"], w["w_mix"], tm=512, tn=1024, name="in_proj_prompt")
    gp = _norm_matmul(xp, w["norm1_g"], w["w_gate"], w["b_gate"], tm=512, tn=1024, name="gates_prompt")
    a_pool, pool_p = _pool_prompt(zp, w["pool_w"], w["pool_scale"], batch=bp, seq=lp, rows=512)
    a_ssm, re_p, im_p = _ssm_prompt(zp, w["ssm"], batch=bp, seq=lp, rows=256)
    a_xa = _xa_prompt(zp, kv, batch=bp, seq=lp, rows=512)
    y_prompt = _finish_layer(xp, zp, gp, a_pool, a_ssm, a_xa, w, final_norm_g).reshape(bp, lp, d)

    xs = x_sample.reshape(bs * ls, d)
    zs = _norm_matmul(xs, w["norm1_g"], w["w_mix"], tm=512, tn=1024, name="in_proj_sample")
    gs = _norm_matmul(xs, w["norm1_g"], w["w_gate"], w["b_gate"], tm=512, tn=1024, name="gates_sample")
    a_pool, pool_s = _pool_sample(zs, state_pool[0], w["pool_w"], w["pool_scale"], batch=bs, seq=ls, bb=16,
                                  n_prev=n_prev_sample)
    a_ssm, re_s, im_s = _ssm_sample(zs, state_ssm_re[0].reshape(bs, SSM_LANES),
                                    state_ssm_im[0].reshape(bs, SSM_LANES), w["ssm"], batch=bs, seq=ls, bb=32)
    a_xa = _xa_sample(zs, cache_mem_k[0].reshape(bs, N_MEM, XA_WIDTH), cache_mem_v[0].reshape(bs, N_MEM, XA_WIDTH),
                      batch=bs, seq=ls, bb=4)
    y_sample = _finish_layer(xs, zs, gs, a_pool, a_ssm, a_xa, w, final_norm_g).reshape(bs, ls, d)

    state_shape = (1, -1, SSM_GROUPS, SSM_STATE)
    mem_shape = (1, bp, N_MEM, XA_HEADS, XA_HEAD_DIM)
    return (y_prompt, y_sample,
            pool_p[None], re_p.reshape(state_shape), im_p.reshape(state_shape),
            kv[:, :XA_WIDTH].reshape(mem_shape), kv[:, XA_WIDTH:].reshape(mem_shape),
            pool_s[None], re_s.reshape(state_shape), im_s.reshape(state_shape))
```

```python
import functools
import math

import jax
import jax.numpy as jnp
from jax import lax
from jax.experimental import pallas as pl
from jax.experimental.pallas import tpu as pltpu

F32 = jnp.float32
BF16 = jnp.bfloat16

D_MODEL = 2048
POOL_WINDOWS = (2, 4, 8, 16)
POOL_GROUP_W = 256
POOL_WIDTH = 1024
POOL_BUF = 15
SSM_WIDTH = 1024
SSM_GROUPS = 64
SSM_GROUP_IN = 16
SSM_STATE = 64
SSM_LANES = SSM_GROUPS * SSM_STATE
SSM_CHUNKS = 4
SSM_CHUNK_GROUPS = SSM_GROUPS // SSM_CHUNKS
SSM_CHUNK_IN = SSM_WIDTH // SSM_CHUNKS
SSM_CHUNK_LANES = SSM_LANES // SSM_CHUNKS
XA_HEADS = 4
XA_HEAD_DIM = 256
XA_WIDTH = 1024
N_MEM = 256
N_BRANCH = 3
D_FF = 4 * D_MODEL
EPS = 1e-6
PAST_LEN = 16384

SUBLANES = 8
LANES = 128
XA_HALVES = XA_HEAD_DIM // LANES
POOL_HIST = 2 * SUBLANES
SCAN_STEPS = (1, 2, 4)
SCAN_LANE_BLOCK = 512
SSM_PROMPT_ROWS = 256
VMEM_LIMIT = 56 * 1024 * 1024


def _params(*sem):
    return pltpu.CompilerParams(dimension_semantics=sem, vmem_limit_bytes=VMEM_LIMIT)


def _resident(shape):
    nd = len(shape)
    return pl.BlockSpec(shape, lambda *_: (0,) * nd, pipeline_mode=pl.Buffered(1))


def _rms_scale(x):
    return lax.rsqrt(jnp.mean(x * x, axis=-1, keepdims=True) + EPS)


def _dot(a, b):
    return jnp.dot(a, b, preferred_element_type=F32)


def _norm_matmul_kernel(*refs, gate_bias):
    if gate_bias:
        x_ref, g_ref, w_ref, b_ref, o_ref, xn_ref = refs
    else:
        x_ref, g_ref, w_ref, o_ref, xn_ref = refs

    @pl.when(pl.program_id(1) == 0)
    def _():
        x = x_ref[...]
        xn_ref[...] = ((x * _rms_scale(x)) * g_ref[...]).astype(BF16)

    acc = _dot(xn_ref[...], w_ref[...])
    if gate_bias:
        acc = jax.nn.sigmoid(acc + b_ref[...])
    o_ref[...] = acc


def _norm_matmul(x, g, w, bias=None, *, tm, tn, name):
    m, d = x.shape
    n = w.shape[1]
    in_specs = [pl.BlockSpec((tm, d), lambda i, j: (i, 0)),
                pl.BlockSpec((1, d), lambda i, j: (0, 0)),
                pl.BlockSpec((d, tn), lambda i, j: (0, j))]
    args = [x, g.reshape(1, d), w]
    if bias is not None:
        in_specs.append(pl.BlockSpec((1, tn), lambda i, j: (0, j)))
        args.append(bias.reshape(1, n))
    return pl.pallas_call(
        functools.partial(_norm_matmul_kernel, gate_bias=bias is not None),
        grid=(m // tm, n // tn),
        in_specs=in_specs,
        out_specs=pl.BlockSpec((tm, tn), lambda i, j: (i, j)),
        out_shape=jax.ShapeDtypeStruct((m, n), F32),
        scratch_shapes=[pltpu.VMEM((tm, d), BF16)],
        compiler_params=_params("parallel", "arbitrary"),
        name=name,
    )(*args)


def _window_means(e_ref, row0, rows, first_pos):
    outs = []
    for k, wdw in enumerate(POOL_WINDOWS):
        lo = k * POOL_GROUP_W
        cur = e_ref[row0:row0 + rows, lo:lo + POOL_GROUP_W]
        s = cur
        for d in range(1, wdw):
            s = s + e_ref[row0 - d:row0 - d + rows, lo:lo + POOL_GROUP_W]
        pos = lax.broadcasted_iota(jnp.int32, (rows, POOL_GROUP_W), 0) + first_pos
        cnt = jnp.minimum(pos, wdw).astype(F32)
        outs.append(s / cnt - cur)
    return outs


def _pool_mix(pooled, k, pw_ref, ps_ref):
    lo = k * POOL_GROUP_W
    mixed = _dot(pooled.astype(BF16), pw_ref[k])
    return (mixed * ps_ref[:, lo:lo + POOL_GROUP_W]).astype(BF16)


def _pool_prompt_kernel(u_ref, pw_ref, ps_ref, o_ref, st_ref, e_ref, *, rows):
    l = pl.program_id(1)

    @pl.when(l == 0)
    def _():
        e_ref[0:POOL_HIST, :] = jnp.zeros((POOL_HIST, POOL_WIDTH), F32)

    @pl.when(l > 0)
    def _():
        e_ref[0:POOL_HIST, :] = e_ref[rows:rows + POOL_HIST, :]

    e_ref[POOL_HIST:POOL_HIST + rows, :] = u_ref[...]
    pooled = _window_means(e_ref, POOL_HIST, rows, l * rows + 1)
    for k, p in enumerate(pooled):
        o_ref[:, k * POOL_GROUP_W:(k + 1) * POOL_GROUP_W] = _pool_mix(p, k, pw_ref, ps_ref)

    @pl.when(l == pl.num_programs(1) - 1)
    def _():
        st_ref[0] = e_ref[POOL_HIST + rows - POOL_BUF:POOL_HIST + rows, :]


def _pool_prompt(z, pool_w, pool_scale, *, batch, seq, rows):
    nl = seq // rows
    return pl.pallas_call(
        functools.partial(_pool_prompt_kernel, rows=rows),
        grid=(batch, nl),
        in_specs=[pl.BlockSpec((rows, POOL_WIDTH), lambda b, l: (b * nl + l, 0)),
                  _resident(pool_w.shape), _resident(pool_scale.shape)],
        out_specs=[pl.BlockSpec((rows, POOL_WIDTH), lambda b, l: (b * nl + l, 0)),
                   pl.BlockSpec((1, POOL_BUF, POOL_WIDTH), lambda b, l: (b, 0, 0))],
        out_shape=[jax.ShapeDtypeStruct((batch * seq, POOL_WIDTH), BF16),
                   jax.ShapeDtypeStruct((batch, POOL_BUF, POOL_WIDTH), F32)],
        scratch_shapes=[pltpu.VMEM((rows + POOL_HIST, POOL_WIDTH), F32)],
        compiler_params=_params("parallel", "arbitrary"),
        name="pool_prompt",
    )(z, pool_w, pool_scale)


def _pool_sample_kernel(u_ref, prev_ref, pw_ref, ps_ref, o_ref, st_ref, e_ref, p_ref, *, bb, seq, n_prev):
    span = POOL_HIST + seq
    for b in range(bb):
        top = b * span + POOL_HIST
        e_ref[top - POOL_BUF:top, :] = prev_ref[b]
        e_ref[top:top + seq, :] = u_ref[b * seq:(b + 1) * seq, :]
    for b in range(bb):
        top = b * span + POOL_HIST
        st_ref[b] = e_ref[top + seq - POOL_BUF:top + seq, :]
        pooled = _window_means(e_ref, top, seq, n_prev + 1)
        for k, p in enumerate(pooled):
            p_ref[b * seq:(b + 1) * seq, k * POOL_GROUP_W:(k + 1) * POOL_GROUP_W] = p
    for k in range(len(POOL_WINDOWS)):
        lo = k * POOL_GROUP_W
        o_ref[:, lo:lo + POOL_GROUP_W] = _pool_mix(p_ref[:, lo:lo + POOL_GROUP_W], k, pw_ref, ps_ref)


def _pool_sample(z, prev, pool_w, pool_scale, *, batch, seq, bb, n_prev):
    return pl.pallas_call(
        functools.partial(_pool_sample_kernel, bb=bb, seq=seq, n_prev=n_prev),
        grid=(batch // bb,),
        in_specs=[pl.BlockSpec((bb * seq, POOL_WIDTH), lambda i: (i, 0)),
                  pl.BlockSpec((bb, POOL_BUF, POOL_WIDTH), lambda i: (i, 0, 0)),
                  _resident(pool_w.shape), _resident(pool_scale.shape)],
        out_specs=[pl.BlockSpec((bb * seq, POOL_WIDTH), lambda i: (i, 0)),
                   pl.BlockSpec((bb, POOL_BUF, POOL_WIDTH), lambda i: (i, 0, 0))],
        out_shape=[jax.ShapeDtypeStruct((batch * seq, POOL_WIDTH), BF16),
                   jax.ShapeDtypeStruct((batch, POOL_BUF, POOL_WIDTH), F32)],
        scratch_shapes=[pltpu.VMEM((bb * (POOL_HIST + seq), POOL_WIDTH), F32),
                        pltpu.VMEM((bb * seq, POOL_WIDTH), F32)],
        compiler_params=_params("parallel"),
        name="pool_sample",
    )(z, prev, pool_w, pool_scale)


def _gather_rows(src_ref, tile_ref, dst_ref, groups, seg_len):
    n_tiles = src_ref.shape[1] // LANES
    for c in range(n_tiles):
        tile_ref[c] = src_ref[:, c * LANES:(c + 1) * LANES]
    for g in range(groups):
        for j in range(seg_len):
            r = (g * seg_len + j) * SUBLANES
            for c in range(n_tiles):
                dst_ref[r:r + SUBLANES, c * LANES:(c + 1) * LANES] = (
                    tile_ref[c, pl.ds(g * SUBLANES * seg_len + j, SUBLANES, stride=seg_len), :])


def _scatter_rows(y, tile_ref, tile0, groups, seg_len):
    for g in range(groups):
        for j in range(seg_len):
            r = (g * seg_len + j) * SUBLANES
            for c in range(y.shape[1] // LANES):
                tile_ref[tile0 + c, pl.ds(g * SUBLANES * seg_len + j, SUBLANES, stride=seg_len), :] = (
                    y[r:r + SUBLANES, c * LANES:(c + 1) * LANES])


def _ssm_input_map(up_ref, wbr_ref, wbi_ref, hr_ref, hi_ref):
    for c in range(SSM_CHUNKS):
        uc = up_ref[:, c * SSM_CHUNK_IN:(c + 1) * SSM_CHUNK_IN].astype(BF16)
        lanes = slice(c * SSM_CHUNK_LANES, (c + 1) * SSM_CHUNK_LANES)
        hr_ref[:, lanes] = _dot(uc, wbr_ref[c])
        hi_ref[:, lanes] = _dot(uc, wbi_ref[c])


def _cmul_add(ar, ai, hr, hi, xr, xi):
    return xr + (ar * hr - ai * hi), xi + (ar * hi + ai * hr)


def _scan_rows(xr, xi, coef, car_r, car_i):
    for n, s in enumerate(SCAN_STEPS):
        xr, xi = _cmul_add(coef[2 * n], coef[2 * n + 1], pltpu.roll(xr, s, 0), pltpu.roll(xi, s, 0), xr, xi)
    return _cmul_add(coef[2 * len(SCAN_STEPS)], coef[2 * len(SCAN_STEPS) + 1], car_r, car_i, xr, xi)


def _last_row(x):
    return jnp.broadcast_to(x[SUBLANES - 1:SUBLANES, :], x.shape)


def _ssm_output(u_ref, hbr_ref, hbi_ref, wcr_ref, wci_ref, d_ref, gw_ref, gb_ref, tile_ref, zg_ref, o_ref,
                groups, seg_len):
    for c in range(SSM_CHUNKS):
        lanes = slice(c * SSM_CHUNK_LANES, (c + 1) * SSM_CHUNK_LANES)
        y = _dot(hbr_ref[:, lanes], wcr_ref[c]) - _dot(hbi_ref[:, lanes], wci_ref[c])
        _scatter_rows(y, tile_ref, c * (SSM_CHUNK_IN // LANES), groups, seg_len)
    for c in range(SSM_WIDTH // LANES):
        cols = slice(c * LANES, (c + 1) * LANES)
        zg_ref[:, cols] = jax.nn.gelu(tile_ref[c] + d_ref[:, cols] * u_ref[:, cols])
    zg = zg_ref[...]
    gate = jax.nn.sigmoid(_dot(zg.astype(BF16), gw_ref[...]) + gb_ref[...])
    o_ref[...] = (zg * gate).astype(BF16)


def _ssm_prompt_kernel(u_ref, wbr_ref, wbi_ref, wcr_ref, wci_ref, a_ref, seg_ref, pw_ref, d_ref, gw_ref, gb_ref,
                       o_ref, sr_ref, si_ref, tile_ref, up_ref, hr_ref, hi_ref, hbr_ref, hbi_ref, zg_ref, car_ref,
                       *, rows):
    seg_len = rows // SUBLANES
    width = SCAN_LANE_BLOCK

    @pl.when(pl.program_id(1) == 0)
    def _():
        car_ref[...] = jnp.zeros(car_ref.shape, F32)

    _gather_rows(u_ref, tile_ref, up_ref, 1, seg_len)
    _ssm_input_map(up_ref, wbr_ref, wbi_ref, hr_ref, hi_ref)

    for lb in range(SSM_LANES // width):
        lanes = slice(lb * width, (lb + 1) * width)
        a_r, a_i = a_ref[0, :, lanes], a_ref[1, :, lanes]

        def local_step(j, h):
            r0 = pl.multiple_of(j * SUBLANES, SUBLANES)
            nr, ni = _cmul_add(a_r, a_i, h[0], h[1], hr_ref[pl.ds(r0, SUBLANES), lanes],
                               hi_ref[pl.ds(r0, SUBLANES), lanes])
            hr_ref[pl.ds(r0, SUBLANES), lanes] = nr
            hi_ref[pl.ds(r0, SUBLANES), lanes] = ni
            return nr, ni

        zero = jnp.zeros((SUBLANES, width), F32)
        loc_r, loc_i = lax.fori_loop(0, seg_len, local_step, (zero, zero), unroll=4)

        car_r, car_i = car_ref[0, :, lanes], car_ref[1, :, lanes]
        seg = tuple(seg_ref[n, :, lanes] for n in range(2 * len(SCAN_STEPS) + 2))
        end_r, end_i = _scan_rows(loc_r, loc_i, seg, car_r, car_i)
        car_ref[0, :, lanes] = _last_row(end_r)
        car_ref[1, :, lanes] = _last_row(end_i)
        sr_ref[0, :, lanes] = end_r[SUBLANES - 1:SUBLANES, :]
        si_ref[0, :, lanes] = end_i[SUBLANES - 1:SUBLANES, :]
        first = lax.broadcasted_iota(jnp.int32, (SUBLANES, width), 0) == 0
        in_r = jnp.where(first, car_r, pltpu.roll(end_r, 1, 0))
        in_i = jnp.where(first, car_i, pltpu.roll(end_i, 1, 0))

        def fix_pair(jj, carry):
            out_r, out_i = [], []
            for d in range(2):
                j = 2 * jj + d
                r0 = pl.multiple_of(j * SUBLANES, SUBLANES)
                p_r = jnp.broadcast_to(pw_ref[0, pl.ds(j, 1), lanes], (SUBLANES, width))
                p_i = jnp.broadcast_to(pw_ref[1, pl.ds(j, 1), lanes], (SUBLANES, width))
                xr, xi = _cmul_add(p_r, p_i, in_r, in_i, hr_ref[pl.ds(r0, SUBLANES), lanes],
                                   hi_ref[pl.ds(r0, SUBLANES), lanes])
                out_r.append(xr)
                out_i.append(xi)
            r16 = pl.multiple_of(jj * 2 * SUBLANES, 2 * SUBLANES)
            hbr_ref[pl.ds(r16, 2 * SUBLANES), lanes] = jnp.concatenate(out_r, axis=0).astype(BF16)
            hbi_ref[pl.ds(r16, 2 * SUBLANES), lanes] = jnp.concatenate(out_i, axis=0).astype(BF16)
            return carry

        lax.fori_loop(0, seg_len // 2, fix_pair, 0, unroll=2)

    _ssm_output(u_ref, hbr_ref, hbi_ref, wcr_ref, wci_ref, d_ref, gw_ref, gb_ref, tile_ref, zg_ref, o_ref,
                1, seg_len)


def _ssm_common_specs(w):
    return [_resident(w["wb_re"].shape), _resident(w["wb_im"].shape),
            _resident(w["wc_re"].shape), _resident(w["wc_im"].shape), _resident(w["a"].shape)]


def _ssm_tail_specs(w):
    return [_resident(w["d"].shape), _resident(w["glu_w"].shape), _resident(w["glu_b"].shape)]


def _ssm_scratch(rows):
    return [pltpu.VMEM((SSM_WIDTH // LANES, rows, LANES), F32), pltpu.VMEM((rows, SSM_WIDTH), F32),
            pltpu.VMEM((rows, SSM_LANES), F32), pltpu.VMEM((rows, SSM_LANES), F32),
            pltpu.VMEM((rows, SSM_LANES), BF16), pltpu.VMEM((rows, SSM_LANES), BF16),
            pltpu.VMEM((rows, SSM_WIDTH), F32)]


def _ssm_prompt(z, w, *, batch, seq, rows):
    nl = seq // rows
    return pl.pallas_call(
        functools.partial(_ssm_prompt_kernel, rows=rows),
        grid=(batch, nl),
        in_specs=([pl.BlockSpec((rows, SSM_WIDTH), lambda b, l: (b * nl + l, 1))] + _ssm_common_specs(w)
                  + [_resident(w["seg"].shape), _resident(w["pw"].shape)] + _ssm_tail_specs(w)),
        out_specs=[pl.BlockSpec((rows, SSM_WIDTH), lambda b, l: (b * nl + l, 0)),
                   pl.BlockSpec((1, 1, SSM_LANES), lambda b, l: (b, 0, 0)),
                   pl.BlockSpec((1, 1, SSM_LANES), lambda b, l: (b, 0, 0))],
        out_shape=[jax.ShapeDtypeStruct((batch * seq, SSM_WIDTH), BF16),
                   jax.ShapeDtypeStruct((batch, 1, SSM_LANES), F32),
                   jax.ShapeDtypeStruct((batch, 1, SSM_LANES), F32)],
        scratch_shapes=_ssm_scratch(rows) + [pltpu.VMEM((2, SUBLANES, SSM_LANES), F32)],
        compiler_params=_params("parallel", "arbitrary"),
        name="ssm_prompt",
    )(z, w["wb_re"], w["wb_im"], w["wc_re"], w["wc_im"], w["a"], w["seg"], w["pw"], w["d"], w["glu_w"], w["glu_b"])


def _ssm_sample_kernel(u_ref, h0r_ref, h0i_ref, wbr_ref, wbi_ref, wcr_ref, wci_ref, a_ref, d_ref, gw_ref, gb_ref,
                       o_ref, sr_ref, si_ref, tile_ref, up_ref, hr_ref, hi_ref, hbr_ref, hbi_ref, zg_ref, *, bb, seq):
    groups = bb // SUBLANES
    width = SCAN_LANE_BLOCK
    _gather_rows(u_ref, tile_ref, up_ref, groups, seq)
    _ssm_input_map(up_ref, wbr_ref, wbi_ref, hr_ref, hi_ref)

    for lb in range(SSM_LANES // width):
        lanes = slice(lb * width, (lb + 1) * width)
        a_r, a_i = a_ref[0, :, lanes], a_ref[1, :, lanes]
        for g in range(groups):
            seqs = slice(g * SUBLANES, (g + 1) * SUBLANES)
            h_r, h_i = h0r_ref[seqs, lanes], h0i_ref[seqs, lanes]
            for jj in range(seq // 2):
                out_r, out_i = [], []
                for d in range(2):
                    r0 = (g * seq + 2 * jj + d) * SUBLANES
                    h_r, h_i = _cmul_add(a_r, a_i, h_r, h_i, hr_ref[r0:r0 + SUBLANES, lanes],
                                         hi_ref[r0:r0 + SUBLANES, lanes])
                    out_r.append(h_r)
                    out_i.append(h_i)
                r16 = (g * seq + 2 * jj) * SUBLANES
                hbr_ref[r16:r16 + 2 * SUBLANES, lanes] = jnp.concatenate(out_r, axis=0).astype(BF16)
                hbi_ref[r16:r16 + 2 * SUBLANES, lanes] = jnp.concatenate(out_i, axis=0).astype(BF16)
            sr_ref[seqs, lanes] = h_r
            si_ref[seqs, lanes] = h_i

    _ssm_output(u_ref, hbr_ref, hbi_ref, wcr_ref, wci_ref, d_ref, gw_ref, gb_ref, tile_ref, zg_ref, o_ref,
                groups, seq)


def _ssm_sample(z, h_re, h_im, w, *, batch, seq, bb):
    assert seq % 2 == 0 and bb % SUBLANES == 0
    rows = bb * seq
    return pl.pallas_call(
        functools.partial(_ssm_sample_kernel, bb=bb, seq=seq),
        grid=(batch // bb,),
        in_specs=([pl.BlockSpec((rows, SSM_WIDTH), lambda i: (i, 1)),
                   pl.BlockSpec((bb, SSM_LANES), lambda i: (i, 0)),
                   pl.BlockSpec((bb, SSM_LANES), lambda i: (i, 0))] + _ssm_common_specs(w) + _ssm_tail_specs(w)),
        out_specs=[pl.BlockSpec((rows, SSM_WIDTH), lambda i: (i, 0)),
                   pl.BlockSpec((bb, SSM_LANES), lambda i: (i, 0)),
                   pl.BlockSpec((bb, SSM_LANES), lambda i: (i, 0))],
        out_shape=[jax.ShapeDtypeStruct((batch * seq, SSM_WIDTH), BF16),
                   jax.ShapeDtypeStruct((batch, SSM_LANES), F32),
                   jax.ShapeDtypeStruct((batch, SSM_LANES), F32)],
        scratch_shapes=_ssm_scratch(rows),
        compiler_params=_params("parallel"),
        name="ssm_sample",
    )(z, h_re, h_im, w["wb_re"], w["wb_im"], w["wc_re"], w["wc_im"], w["a"], w["d"], w["glu_w"], w["glu_b"])


def _ssm_weights(a_re, a_im, log_dt, b_re, b_im, c_re, c_im, d_skip, glu_w, glu_b, *, seg_len):
    dt = jnp.exp(log_dt)[:, None]
    mag = jnp.exp(dt * a_re)
    abr, abi = mag * jnp.cos(dt * a_im), mag * jnp.sin(dt * a_im)
    den = a_re * a_re + a_im * a_im
    xr, xi = abr - 1.0, abi
    fr = (xr * a_re + xi * a_im) / den
    fi = (xi * a_re - xr * a_im) / den
    bbr = fr[..., None] * b_re - fi[..., None] * b_im
    bbi = fr[..., None] * b_im + fi[..., None] * b_re
    eye = jnp.eye(SSM_CHUNK_GROUPS, dtype=F32)

    def in_map(bb):
        t = bb.transpose(0, 2, 1).reshape(SSM_CHUNKS, SSM_CHUNK_GROUPS, SSM_GROUP_IN, SSM_STATE)
        t = t[:, :, :, None, :] * eye[None, :, None, :, None]
        return t.reshape(SSM_CHUNKS, SSM_CHUNK_IN, SSM_CHUNK_LANES).astype(BF16)

    def out_map(cc):
        t = cc.reshape(SSM_CHUNKS, SSM_CHUNK_GROUPS, SSM_GROUP_IN, SSM_STATE).transpose(0, 1, 3, 2)
        t = t[:, :, :, None, :] * eye[None, :, None, :, None]
        return t.reshape(SSM_CHUNKS, SSM_CHUNK_LANES, SSM_CHUNK_IN).astype(BF16)

    def powers(base, n):
        tr, ti = base
        while tr.shape[0] < n:
            qr, qi = tr[-1:], ti[-1:]
            tr, ti = (jnp.concatenate([tr, tr * qr - ti * qi], axis=0),
                      jnp.concatenate([ti, tr * qi + ti * qr], axis=0))
        return tr, ti

    assert seg_len & (seg_len - 1) == 0
    a_row = (abr.reshape(1, SSM_LANES), abi.reshape(1, SSM_LANES))
    ones = jnp.ones((SUBLANES, 1), F32)
    a_b = jnp.stack([ones * a_row[0], ones * a_row[1]])
    pw_r, pw_i = powers(a_row, seg_len)
    sp_r, sp_i = powers((pw_r[-1:], pw_i[-1:]), SUBLANES)
    row = jnp.arange(SUBLANES)[:, None]
    seg = []
    for s in SCAN_STEPS:
        seg += [jnp.where(row >= s, sp_r[s - 1:s], 0.0), jnp.where(row >= s, sp_i[s - 1:s], 0.0)]
    seg += [sp_r, sp_i]
    pw = jnp.stack([pw_r, pw_i])
    return {"wb_re": in_map(bbr), "wb_im": in_map(bbi), "wc_re": out_map(c_re), "wc_im": out_map(c_im),
            "a": a_b, "seg": jnp.stack(seg), "pw": pw, "d": d_skip.reshape(1, SSM_WIDTH),
            "glu_w": glu_w.astype(BF16), "glu_b": glu_b.reshape(1, SSM_WIDTH)}


def _attend(q, k, v):
    s = lax.dot_general(q, k, (((1,), (1,)), ((), ())), preferred_element_type=F32)
    s = s * (XA_HEAD_DIM ** -0.5)
    p = jnp.exp(s - jnp.max(s, axis=-1, keepdims=True))
    p = p / jnp.sum(p, axis=-1, keepdims=True)
    return _dot(p.astype(BF16), v)


def _xa_prompt_kernel(q_ref, k_ref, v_ref, o_ref):
    for h in range(XA_HEADS):
        cols = slice(h * XA_HEAD_DIM, (h + 1) * XA_HEAD_DIM)
        o = _attend(q_ref[:, cols].astype(BF16), k_ref[:, cols].astype(BF16), v_ref[:, cols].astype(BF16))
        o_ref[:, cols] = o.astype(BF16)


def _xa_prompt(z, kv, *, batch, seq, rows):
    nl = seq // rows
    return pl.pallas_call(
        _xa_prompt_kernel,
        grid=(batch, nl),
        in_specs=[pl.BlockSpec((rows, XA_WIDTH), lambda b, l: (b * nl + l, 2)),
                  pl.BlockSpec((N_MEM, XA_WIDTH), lambda b, l: (b, 0)),
                  pl.BlockSpec((N_MEM, XA_WIDTH), lambda b, l: (b, 1))],
        out_specs=pl.BlockSpec((rows, XA_WIDTH), lambda b, l: (b * nl + l, 0)),
        out_shape=jax.ShapeDtypeStruct((batch * seq, XA_WIDTH), BF16),
        compiler_params=_params("parallel", "arbitrary"),
        name="xattn_prompt",
    )(z, kv, kv)


def _kv_rows(cache):
    b = cache.shape[0]
    c = cache.reshape(b, N_MEM, XA_HEADS, XA_HALVES, LANES).transpose(0, 1, 3, 2, 4)
    return c.reshape(b, N_MEM * XA_HALVES * XA_HEADS, LANES)


def _xa_sample_kernel(q_ref, k_ref, v_ref, o_ref, *, bb, seq):
    hl = XA_HEADS * seq
    n_rows = N_MEM * XA_HALVES * XA_HEADS
    col = lax.broadcasted_iota(jnp.int32, (hl, n_rows), 1)
    row = lax.broadcasted_iota(jnp.int32, (hl, n_rows), 0)
    slot = col & (XA_HALVES * XA_HEADS - 1)
    head = row // seq
    first_half = slot == head
    second_half = slot == head + XA_HEADS
    for b in range(bb):
        rows = slice(b * seq, (b + 1) * seq)
        q = q_ref[rows, :].astype(BF16)
        qm = jnp.concatenate([q[:, h * XA_HEAD_DIM + t * LANES:h * XA_HEAD_DIM + (t + 1) * LANES]
                              for t in range(XA_HALVES) for h in range(XA_HEADS)], axis=0)
        g = lax.dot_general(qm, k_ref[b].astype(BF16), (((1,), (1,)), ((), ())), preferred_element_type=F32)
        g0 = jnp.where(first_half, g[:hl], 0.0)
        g1 = jnp.where(second_half, g[hl:], 0.0)
        s = (g0 + pltpu.roll(g1, n_rows - XA_HEADS, 1)) * (XA_HEAD_DIM ** -0.5)
        s = jnp.where(first_half, s, -jnp.inf)
        p = jnp.exp(s - jnp.max(s, axis=-1, keepdims=True))
        p = p / jnp.sum(p, axis=-1, keepdims=True)
        pp = jnp.concatenate([p, pltpu.roll(p, XA_HEADS, 1)], axis=0).astype(BF16)
        o = _dot(pp, v_ref[b].astype(BF16))
        for t in range(XA_HALVES):
            for h in range(XA_HEADS):
                r0 = t * hl + h * seq
                c0 = h * XA_HEAD_DIM + t * LANES
                o_ref[rows, c0:c0 + LANES] = o[r0:r0 + seq].astype(BF16)


def _xa_sample(z, mem_k, mem_v, *, batch, seq, bb):
    n_rows = N_MEM * XA_HALVES * XA_HEADS
    return pl.pallas_call(
        functools.partial(_xa_sample_kernel, bb=bb, seq=seq),
        grid=(batch // bb,),
        in_specs=[pl.BlockSpec((bb * seq, XA_WIDTH), lambda i: (i, 2)),
                  pl.BlockSpec((bb, n_rows, LANES), lambda i: (i, 0, 0)),
                  pl.BlockSpec((bb, n_rows, LANES), lambda i: (i, 0, 0))],
        out_specs=pl.BlockSpec((bb * seq, XA_WIDTH), lambda i: (i, 0)),
        out_shape=jax.ShapeDtypeStruct((batch * seq, XA_WIDTH), BF16),
        compiler_params=_params("parallel"),
        name="xattn_sample",
    )(z, mem_k, mem_v)


def _merge_kernel(ap_ref, as_ref, ax_ref, g0_ref, g1_ref, g2_ref, x_ref, wp_ref, ws_ref, wx_ref, wo_ref, o_ref):
    merged = (g0_ref[...] * _dot(ap_ref[...], wp_ref[...])
              + g1_ref[...] * _dot(as_ref[...], ws_ref[...])
              + g2_ref[...] * _dot(ax_ref[...], wx_ref[...]))
    o_ref[...] = x_ref[...] + _dot(merged.astype(BF16), wo_ref[...])


def _merge(a_pool, a_ssm, a_xa, gates, x, wp, ws, wx, wo, *, tm):
    m, d = x.shape
    act = pl.BlockSpec((tm, a_pool.shape[1]), lambda i: (i, 0))
    return pl.pallas_call(
        _merge_kernel,
        grid=(m // tm,),
        in_specs=[act, act, act,
                  pl.BlockSpec((tm, d), lambda i: (i, 0)),
                  pl.BlockSpec((tm, d), lambda i: (i, 1)),
                  pl.BlockSpec((tm, d), lambda i: (i, 2)),
                  pl.BlockSpec((tm, d), lambda i: (i, 0)),
                  _resident(wp.shape), _resident(ws.shape), _resident(wx.shape), _resident(wo.shape)],
        out_specs=pl.BlockSpec((tm, d), lambda i: (i, 0)),
        out_shape=jax.ShapeDtypeStruct((m, d), F32),
        compiler_params=_params("parallel"),
        name="merge",
    )(a_pool, a_ssm, a_xa, gates, gates, gates, x, wp, ws, wx, wo)


def _mlp_kernel(x_ref, g_ref, w1_ref, w2_ref, gf_ref, o_ref, xn_ref, acc_ref):
    f = pl.program_id(1)

    @pl.when(f == 0)
    def _():
        x = x_ref[...]
        xn_ref[...] = ((x * _rms_scale(x)) * g_ref[...]).astype(BF16)
        acc_ref[...] = x

    h = jnp.square(jnp.maximum(_dot(xn_ref[...], w1_ref[...]), 0.0))
    acc_ref[...] += _dot(h.astype(BF16), w2_ref[...])

    @pl.when(f == pl.num_programs(1) - 1)
    def _():
        y = acc_ref[...]
        o_ref[...] = (y * _rms_scale(y)) * gf_ref[...]


def _mlp(x, g, w1, w2, gf, *, tm, tf):
    m, d = x.shape
    dff = w1.shape[1]
    return pl.pallas_call(
        _mlp_kernel,
        grid=(m // tm, dff // tf),
        in_specs=[pl.BlockSpec((tm, d), lambda i, f: (i, 0)),
                  pl.BlockSpec((1, d), lambda i, f: (0, 0)),
                  pl.BlockSpec((d, tf), lambda i, f: (0, f)),
                  pl.BlockSpec((tf, d), lambda i, f: (f, 0)),
                  pl.BlockSpec((1, d), lambda i, f: (0, 0))],
        out_specs=pl.BlockSpec((tm, d), lambda i, f: (i, 0)),
        out_shape=jax.ShapeDtypeStruct((m, d), F32),
        scratch_shapes=[pltpu.VMEM((tm, d), BF16), pltpu.VMEM((tm, d), F32)],
        compiler_params=_params("parallel", "arbitrary"),
        name="mlp",
    )(x, g.reshape(1, d), w1, w2, gf.reshape(1, d))


def _layer_weights(l, norm1_g, w_in, b_gate, pool_w, pool_scale, pool_proj, ssm_A_re, ssm_A_im, ssm_log_dt,
                   ssm_B_re, ssm_B_im, ssm_C_re, ssm_C_im, ssm_D, ssm_glu_w, ssm_glu_b, ssm_proj,
                   mem_norm_g, xa_wk, xa_wv, xa_wo, w_out, norm2_g, mlp_w1, mlp_w2):
    n_mix = POOL_WIDTH + SSM_WIDTH + XA_WIDTH
    w_in_l = w_in[l].astype(BF16)
    return {
        "norm1_g": norm1_g[l], "w_mix": w_in_l[:, :n_mix], "w_gate": w_in_l[:, n_mix:],
        "b_gate": b_gate[l].reshape(-1),
        "pool_w": pool_w[l].astype(BF16), "pool_scale": pool_scale[l].reshape(1, POOL_WIDTH),
        "pool_proj": pool_proj[l].astype(BF16),
        "ssm": _ssm_weights(ssm_A_re[l], ssm_A_im[l], ssm_log_dt[l], ssm_B_re[l], ssm_B_im[l],
                            ssm_C_re[l], ssm_C_im[l], ssm_D[l], ssm_glu_w[l], ssm_glu_b[l],
                            seg_len=SSM_PROMPT_ROWS // SUBLANES),
        "ssm_proj": ssm_proj[l].astype(BF16),
        "mem_norm_g": mem_norm_g[l],
        "w_kv": jnp.concatenate([xa_wk[l], xa_wv[l]], axis=1).astype(BF16),
        "xa_wo": xa_wo[l].astype(BF16), "w_out": w_out[l].astype(BF16),
        "norm2_g": norm2_g[l], "mlp_w1": mlp_w1[l].astype(BF16), "mlp_w2": mlp_w2[l].astype(BF16),
    }


def _token_tile(m, cap):
    t = min(m, cap)
    assert m % t == 0
    return t


def _finish_layer(x, z, gates, a_pool, a_ssm, a_xa, w, final_g):
    m = x.shape[0]
    x = _merge(a_pool, a_ssm, a_xa, gates, x, w["pool_proj"], w["ssm_proj"], w["xa_wo"], w["w_out"],
               tm=_token_tile(m, 256))
    return _mlp(x, w["norm2_g"], w["mlp_w1"], w["mlp_w2"], final_g, tm=_token_tile(m, 512), tf=1024)


def kernel(x_prompt, x_sample, state_pool, state_ssm_re, state_ssm_im, cache_mem_k, cache_mem_v, mem_prompt,
           norm1_g, w_in, b_gate, pool_w, pool_scale, pool_proj, ssm_A_re, ssm_A_im, ssm_log_dt, ssm_B_re,
           ssm_B_im, ssm_C_re, ssm_C_im, ssm_D, ssm_glu_w, ssm_glu_b, ssm_proj, mem_norm_g, xa_wk, xa_wv,
           xa_wo, w_out, norm2_g, mlp_w1, mlp_w2, final_norm_g):
    depth = norm1_g.shape[0]
    assert depth == 1, "the final norm is fused into the single layer's MLP kernel"
    bp, lp, d = x_prompt.shape
    bs, ls, _ = x_sample.shape
    n_prev_sample = min(POOL_BUF, PAST_LEN)
    w = _layer_weights(0, norm1_g, w_in, b_gate, pool_w, pool_scale, pool_proj, ssm_A_re, ssm_A_im, ssm_log_dt,
                       ssm_B_re, ssm_B_im, ssm_C_re, ssm_C_im, ssm_D, ssm_glu_w, ssm_glu_b, ssm_proj,
                       mem_norm_g, xa_wk, xa_wv, xa_wo, w_out, norm2_g, mlp_w1, mlp_w2)

    xp = x_prompt.reshape(bp * lp, d)
    kv = _norm_matmul(mem_prompt.reshape(bp * N_MEM, d), w["mem_norm_g"], w["w_kv"],
                      tm=_token_tile(bp * N_MEM, 512), tn=1024, name="memory_kv")
    zp = _norm_matmul(xp, w["norm1_g"], w["w_mix"], tm=512, tn=1024, name="in_proj_prompt")
    gp = _norm_matmul(xp, w["norm1_g"], w["w_gate"], w["b_gate"], tm=512, tn=1024, name="gates_prompt")
    a_pool, pool_p = _pool_prompt(zp, w["pool_w"], w["pool_scale"], batch=bp, seq=lp, rows=512)
    a_ssm, re_p, im_p = _ssm_prompt(zp, w["ssm"], batch=bp, seq=lp, rows=SSM_PROMPT_ROWS)
    a_xa = _xa_prompt(zp, kv, batch=bp, seq=lp, rows=512)
    y_prompt = _finish_layer(xp, zp, gp, a_pool, a_ssm, a_xa, w, final_norm_g).reshape(bp, lp, d)

    xs = x_sample.reshape(bs * ls, d)
    zs = _norm_matmul(xs, w["norm1_g"], w["w_mix"], tm=512, tn=1024, name="in_proj_sample")
    gs = _norm_matmul(xs, w["norm1_g"], w["w_gate"], w["b_gate"], tm=512, tn=1024, name="gates_sample")
    a_pool, pool_s = _pool_sample(zs, state_pool[0], w["pool_w"], w["pool_scale"], batch=bs, seq=ls, bb=16,
                                  n_prev=n_prev_sample)
    a_ssm, re_s, im_s = _ssm_sample(zs, state_ssm_re[0].reshape(bs, SSM_LANES),
                                    state_ssm_im[0].reshape(bs, SSM_LANES), w["ssm"], batch=bs, seq=ls, bb=32)
    a_xa = _xa_sample(zs, _kv_rows(cache_mem_k[0]), _kv_rows(cache_mem_v[0]), batch=bs, seq=ls, bb=4)
    y_sample = _finish_layer(xs, zs, gs, a_pool, a_ssm, a_xa, w, final_norm_g).reshape(bs, ls, d)

    state_shape = (1, -1, SSM_GROUPS, SSM_STATE)
    mem_shape = (1, bp, N_MEM, XA_HEADS, XA_HEAD_DIM)
    return (y_prompt, y_sample,
            pool_p[None], re_p.reshape(state_shape), im_p.reshape(state_shape),
            kv[:, :XA_WIDTH].reshape(mem_shape), kv[:, XA_WIDTH:].reshape(mem_shape),
            pool_s[None], re_s.reshape(state_shape), im_s.reshape(state_shape))
```

```python
import functools
import math

import jax
import jax.numpy as jnp
from jax import lax
from jax.experimental import pallas as pl
from jax.experimental.pallas import tpu as pltpu

F32 = jnp.float32
BF16 = jnp.bfloat16

D_MODEL = 2048
POOL_WINDOWS = (2, 4, 8, 16)
POOL_GROUP_W = 256
POOL_WIDTH = 1024
POOL_BUF = 15
SSM_WIDTH = 1024
SSM_GROUPS = 64
SSM_GROUP_IN = 16
SSM_STATE = 64
SSM_LANES = SSM_GROUPS * SSM_STATE
SSM_CHUNKS = 4
SSM_CHUNK_GROUPS = SSM_GROUPS // SSM_CHUNKS
SSM_CHUNK_IN = SSM_WIDTH // SSM_CHUNKS
SSM_CHUNK_LANES = SSM_LANES // SSM_CHUNKS
XA_HEADS = 4
XA_HEAD_DIM = 256
XA_WIDTH = 1024
N_MEM = 256
N_BRANCH = 3
D_FF = 4 * D_MODEL
EPS = 1e-6
PAST_LEN = 16384

SUBLANES = 8
LANES = 128
XA_HALVES = XA_HEAD_DIM // LANES
POOL_HIST = 2 * SUBLANES
SCAN_STEPS = (1, 2, 4)
SCAN_LANE_BLOCK = 512
SSM_PROMPT_ROWS = 256

IN_TM = 1024
MERGE_TM = 256
MLP_TM = 1024
MLP_TF = 512
POOL_PROMPT_ROWS = 512
XA_PROMPT_ROWS = 512
POOL_SAMPLE_SEQS = 16
SSM_SAMPLE_SEQS = 32
XA_SAMPLE_SEQS = 4
VMEM_LIMIT = 56 * 1024 * 1024


def _params(*sem):
    return pltpu.CompilerParams(dimension_semantics=sem, vmem_limit_bytes=VMEM_LIMIT)


def _resident(shape):
    nd = len(shape)
    return pl.BlockSpec(shape, lambda *_: (0,) * nd, pipeline_mode=pl.Buffered(1))


def _rms_scale(x):
    return lax.rsqrt(jnp.mean(x * x, axis=-1, keepdims=True) + EPS)


def _dot(a, b):
    return jnp.dot(a, b, preferred_element_type=F32)


def _norm_to_scratch(x_ref, g_ref, xn_ref):
    x = x_ref[...]
    xn_ref[...] = ((x * _rms_scale(x)) * g_ref[...]).astype(BF16)


def _norm_matmul_kernel(x_ref, g_ref, w_ref, o_ref, xn_ref):
    @pl.when(pl.program_id(1) == 0)
    def _():
        _norm_to_scratch(x_ref, g_ref, xn_ref)

    o_ref[...] = _dot(xn_ref[...], w_ref[...])


def _norm_matmul(x, g, w, *, tm, tn, name):
    m, d = x.shape
    n = w.shape[1]
    return pl.pallas_call(
        _norm_matmul_kernel,
        grid=(m // tm, n // tn),
        in_specs=[pl.BlockSpec((tm, d), lambda i, j: (i, 0)),
                  pl.BlockSpec((1, d), lambda i, j: (0, 0)),
                  pl.BlockSpec((d, tn), lambda i, j: (0, j))],
        out_specs=pl.BlockSpec((tm, tn), lambda i, j: (i, j)),
        out_shape=jax.ShapeDtypeStruct((m, n), F32),
        scratch_shapes=[pltpu.VMEM((tm, d), BF16)],
        compiler_params=_params("parallel", "arbitrary"),
        name=name,
    )(x, g.reshape(1, d), w)


IN_TN = 1024
IN_U_BLOCKS = (POOL_WIDTH + SSM_WIDTH) // IN_TN
IN_Q_BLOCKS = XA_WIDTH // IN_TN
IN_GATE0 = IN_U_BLOCKS + IN_Q_BLOCKS


def _in_proj_kernel(x_ref, g_ref, w_ref, b_ref, u_ref, q_ref, gate_ref, xn_ref):
    j = pl.program_id(1)

    @pl.when(j == 0)
    def _():
        _norm_to_scratch(x_ref, g_ref, xn_ref)

    @pl.when(j < IN_U_BLOCKS)
    def _():
        u_ref[...] = _dot(xn_ref[...], w_ref[...])

    @pl.when(jnp.logical_and(j >= IN_U_BLOCKS, j < IN_GATE0))
    def _():
        q_ref[...] = _dot(xn_ref[...], w_ref[...]).astype(BF16)

    @pl.when(j >= IN_GATE0)
    def _():
        gate_ref[...] = jax.nn.sigmoid(_dot(xn_ref[...], w_ref[...]) + b_ref[...]).astype(BF16)


def _in_proj(x, g, w_in, b_gate, *, tm, name):
    m, d = x.shape
    n = w_in.shape[1]
    n_gate = n // IN_TN - IN_GATE0
    return pl.pallas_call(
        _in_proj_kernel,
        grid=(m // tm, n // IN_TN),
        in_specs=[pl.BlockSpec((tm, d), lambda i, j: (i, 0)),
                  pl.BlockSpec((1, d), lambda i, j: (0, 0)),
                  pl.BlockSpec((d, IN_TN), lambda i, j: (0, j)),
                  pl.BlockSpec((1, IN_TN), lambda i, j: (0, jnp.maximum(j - IN_GATE0, 0)))],
        out_specs=[pl.BlockSpec((tm, IN_TN), lambda i, j: (i, jnp.minimum(j, IN_U_BLOCKS - 1))),
                   pl.BlockSpec((tm, IN_TN), lambda i, j: (i, jnp.clip(j - IN_U_BLOCKS, 0, IN_Q_BLOCKS - 1))),
                   pl.BlockSpec((tm, IN_TN), lambda i, j: (i, jnp.maximum(j - IN_GATE0, 0)))],
        out_shape=[jax.ShapeDtypeStruct((m, IN_U_BLOCKS * IN_TN), F32),
                   jax.ShapeDtypeStruct((m, IN_Q_BLOCKS * IN_TN), BF16),
                   jax.ShapeDtypeStruct((m, n_gate * IN_TN), BF16)],
        scratch_shapes=[pltpu.VMEM((tm, d), BF16)],
        compiler_params=_params("parallel", "arbitrary"),
        name=name,
    )(x, g.reshape(1, d), w_in, b_gate.reshape(1, n_gate * IN_TN))


def _window_means(e_ref, row0, rows, first_pos):
    outs = []
    for k, wdw in enumerate(POOL_WINDOWS):
        lo = k * POOL_GROUP_W
        cur = e_ref[row0:row0 + rows, lo:lo + POOL_GROUP_W]
        s = cur
        for d in range(1, wdw):
            s = s + e_ref[row0 - d:row0 - d + rows, lo:lo + POOL_GROUP_W]
        pos = lax.broadcasted_iota(jnp.int32, (rows, POOL_GROUP_W), 0) + first_pos
        cnt = jnp.minimum(pos, wdw).astype(F32)
        outs.append(s / cnt - cur)
    return outs


def _pool_mix(pooled, k, pw_ref, ps_ref):
    lo = k * POOL_GROUP_W
    mixed = _dot(pooled.astype(BF16), pw_ref[k])
    return (mixed * ps_ref[:, lo:lo + POOL_GROUP_W]).astype(BF16)


def _pool_prompt_kernel(u_ref, pw_ref, ps_ref, o_ref, st_ref, e_ref, *, rows):
    l = pl.program_id(1)

    @pl.when(l == 0)
    def _():
        e_ref[0:POOL_HIST, :] = jnp.zeros((POOL_HIST, POOL_WIDTH), F32)

    @pl.when(l > 0)
    def _():
        e_ref[0:POOL_HIST, :] = e_ref[rows:rows + POOL_HIST, :]

    e_ref[POOL_HIST:POOL_HIST + rows, :] = u_ref[...]
    pooled = _window_means(e_ref, POOL_HIST, rows, l * rows + 1)
    for k, p in enumerate(pooled):
        o_ref[:, k * POOL_GROUP_W:(k + 1) * POOL_GROUP_W] = _pool_mix(p, k, pw_ref, ps_ref)

    @pl.when(l == pl.num_programs(1) - 1)
    def _():
        st_ref[0] = e_ref[POOL_HIST + rows - POOL_BUF:POOL_HIST + rows, :]


def _pool_prompt(z, pool_w, pool_scale, *, batch, seq, rows):
    nl = seq // rows
    return pl.pallas_call(
        functools.partial(_pool_prompt_kernel, rows=rows),
        grid=(batch, nl),
        in_specs=[pl.BlockSpec((rows, POOL_WIDTH), lambda b, l: (b * nl + l, 0)),
                  _resident(pool_w.shape), _resident(pool_scale.shape)],
        out_specs=[pl.BlockSpec((rows, POOL_WIDTH), lambda b, l: (b * nl + l, 0)),
                   pl.BlockSpec((1, POOL_BUF, POOL_WIDTH), lambda b, l: (b, 0, 0))],
        out_shape=[jax.ShapeDtypeStruct((batch * seq, POOL_WIDTH), BF16),
                   jax.ShapeDtypeStruct((batch, POOL_BUF, POOL_WIDTH), F32)],
        scratch_shapes=[pltpu.VMEM((rows + POOL_HIST, POOL_WIDTH), F32)],
        compiler_params=_params("parallel", "arbitrary"),
        name="pool_prompt",
    )(z, pool_w, pool_scale)


def _pool_sample_kernel(u_ref, prev_ref, pw_ref, ps_ref, o_ref, st_ref, e_ref, p_ref, *, bb, seq, n_prev):
    span = POOL_HIST + seq
    for b in range(bb):
        top = b * span + POOL_HIST
        e_ref[top - POOL_BUF:top, :] = prev_ref[b]
        e_ref[top:top + seq, :] = u_ref[b * seq:(b + 1) * seq, :]
    for b in range(bb):
        top = b * span + POOL_HIST
        st_ref[b] = e_ref[top + seq - POOL_BUF:top + seq, :]
        pooled = _window_means(e_ref, top, seq, n_prev + 1)
        for k, p in enumerate(pooled):
            p_ref[b * seq:(b + 1) * seq, k * POOL_GROUP_W:(k + 1) * POOL_GROUP_W] = p
    for k in range(len(POOL_WINDOWS)):
        lo = k * POOL_GROUP_W
        o_ref[:, lo:lo + POOL_GROUP_W] = _pool_mix(p_ref[:, lo:lo + POOL_GROUP_W], k, pw_ref, ps_ref)


def _pool_sample(z, prev, pool_w, pool_scale, *, batch, seq, bb, n_prev):
    return pl.pallas_call(
        functools.partial(_pool_sample_kernel, bb=bb, seq=seq, n_prev=n_prev),
        grid=(batch // bb,),
        in_specs=[pl.BlockSpec((bb * seq, POOL_WIDTH), lambda i: (i, 0)),
                  pl.BlockSpec((bb, POOL_BUF, POOL_WIDTH), lambda i: (i, 0, 0)),
                  _resident(pool_w.shape), _resident(pool_scale.shape)],
        out_specs=[pl.BlockSpec((bb * seq, POOL_WIDTH), lambda i: (i, 0)),
                   pl.BlockSpec((bb, POOL_BUF, POOL_WIDTH), lambda i: (i, 0, 0))],
        out_shape=[jax.ShapeDtypeStruct((batch * seq, POOL_WIDTH), BF16),
                   jax.ShapeDtypeStruct((batch, POOL_BUF, POOL_WIDTH), F32)],
        scratch_shapes=[pltpu.VMEM((bb * (POOL_HIST + seq), POOL_WIDTH), F32),
                        pltpu.VMEM((bb * seq, POOL_WIDTH), F32)],
        compiler_params=_params("parallel"),
        name="pool_sample",
    )(z, prev, pool_w, pool_scale)


def _gather_rows(src_ref, tile_ref, dst_ref, groups, seg_len):
    n_tiles = src_ref.shape[1] // LANES
    for c in range(n_tiles):
        tile_ref[c] = src_ref[:, c * LANES:(c + 1) * LANES]
    for g in range(groups):
        for j in range(seg_len):
            r = (g * seg_len + j) * SUBLANES
            for c in range(n_tiles):
                dst_ref[r:r + SUBLANES, c * LANES:(c + 1) * LANES] = (
                    tile_ref[c, pl.ds(g * SUBLANES * seg_len + j, SUBLANES, stride=seg_len), :])


def _scatter_rows(y, tile_ref, tile0, groups, seg_len):
    for g in range(groups):
        for j in range(seg_len):
            r = (g * seg_len + j) * SUBLANES
            for c in range(y.shape[1] // LANES):
                tile_ref[tile0 + c, pl.ds(g * SUBLANES * seg_len + j, SUBLANES, stride=seg_len), :] = (
                    y[r:r + SUBLANES, c * LANES:(c + 1) * LANES])


def _ssm_input_map(up_ref, wbr_ref, wbi_ref, hr_ref, hi_ref):
    for c in range(SSM_CHUNKS):
        uc = up_ref[:, c * SSM_CHUNK_IN:(c + 1) * SSM_CHUNK_IN].astype(BF16)
        lanes = slice(c * SSM_CHUNK_LANES, (c + 1) * SSM_CHUNK_LANES)
        hr_ref[:, lanes] = _dot(uc, wbr_ref[c])
        hi_ref[:, lanes] = _dot(uc, wbi_ref[c])


def _cmul_add(ar, ai, hr, hi, xr, xi):
    return xr + (ar * hr - ai * hi), xi + (ar * hi + ai * hr)


def _scan_rows(xr, xi, coef, car_r, car_i):
    for n, s in enumerate(SCAN_STEPS):
        xr, xi = _cmul_add(coef[2 * n], coef[2 * n + 1], pltpu.roll(xr, s, 0), pltpu.roll(xi, s, 0), xr, xi)
    return _cmul_add(coef[2 * len(SCAN_STEPS)], coef[2 * len(SCAN_STEPS) + 1], car_r, car_i, xr, xi)


def _last_row(x):
    return jnp.broadcast_to(x[SUBLANES - 1:SUBLANES, :], x.shape)


def _ssm_output(u_ref, hbr_ref, hbi_ref, wcr_ref, wci_ref, d_ref, gw_ref, gb_ref, tile_ref, zg_ref, o_ref,
                groups, seg_len):
    for c in range(SSM_CHUNKS):
        lanes = slice(c * SSM_CHUNK_LANES, (c + 1) * SSM_CHUNK_LANES)
        y = _dot(hbr_ref[:, lanes], wcr_ref[c]) - _dot(hbi_ref[:, lanes], wci_ref[c])
        _scatter_rows(y, tile_ref, c * (SSM_CHUNK_IN // LANES), groups, seg_len)
    for c in range(SSM_WIDTH // LANES):
        cols = slice(c * LANES, (c + 1) * LANES)
        zg_ref[:, cols] = jax.nn.gelu(tile_ref[c] + d_ref[:, cols] * u_ref[:, cols])
    zg = zg_ref[...]
    gate = jax.nn.sigmoid(_dot(zg.astype(BF16), gw_ref[...]) + gb_ref[...])
    o_ref[...] = (zg * gate).astype(BF16)


def _ssm_prompt_kernel(u_ref, wbr_ref, wbi_ref, wcr_ref, wci_ref, a_ref, seg_ref, pw_ref, d_ref, gw_ref, gb_ref,
                       o_ref, sr_ref, si_ref, tile_ref, up_ref, hr_ref, hi_ref, hbr_ref, hbi_ref, zg_ref, car_ref,
                       *, rows):
    seg_len = rows // SUBLANES
    width = SCAN_LANE_BLOCK

    @pl.when(pl.program_id(1) == 0)
    def _():
        car_ref[...] = jnp.zeros(car_ref.shape, F32)

    _gather_rows(u_ref, tile_ref, up_ref, 1, seg_len)
    _ssm_input_map(up_ref, wbr_ref, wbi_ref, hr_ref, hi_ref)

    for lb in range(SSM_LANES // width):
        lanes = slice(lb * width, (lb + 1) * width)
        a_r, a_i = a_ref[0, :, lanes], a_ref[1, :, lanes]

        def local_step(j, h):
            r0 = pl.multiple_of(j * SUBLANES, SUBLANES)
            nr, ni = _cmul_add(a_r, a_i, h[0], h[1], hr_ref[pl.ds(r0, SUBLANES), lanes],
                               hi_ref[pl.ds(r0, SUBLANES), lanes])
            hr_ref[pl.ds(r0, SUBLANES), lanes] = nr
            hi_ref[pl.ds(r0, SUBLANES), lanes] = ni
            return nr, ni

        zero = jnp.zeros((SUBLANES, width), F32)
        loc_r, loc_i = lax.fori_loop(0, seg_len, local_step, (zero, zero), unroll=True)

        car_r, car_i = car_ref[0, :, lanes], car_ref[1, :, lanes]
        seg = tuple(seg_ref[n, :, lanes] for n in range(2 * len(SCAN_STEPS) + 2))
        end_r, end_i = _scan_rows(loc_r, loc_i, seg, car_r, car_i)
        car_ref[0, :, lanes] = _last_row(end_r)
        car_ref[1, :, lanes] = _last_row(end_i)
        sr_ref[0, :, lanes] = end_r[SUBLANES - 1:SUBLANES, :]
        si_ref[0, :, lanes] = end_i[SUBLANES - 1:SUBLANES, :]
        first = lax.broadcasted_iota(jnp.int32, (SUBLANES, width), 0) == 0
        in_r = jnp.where(first, car_r, pltpu.roll(end_r, 1, 0))
        in_i = jnp.where(first, car_i, pltpu.roll(end_i, 1, 0))

        def fix_pair(jj, carry):
            out_r, out_i = [], []
            for d in range(2):
                j = 2 * jj + d
                r0 = pl.multiple_of(j * SUBLANES, SUBLANES)
                p_r = jnp.broadcast_to(pw_ref[0, pl.ds(j, 1), lanes], (SUBLANES, width))
                p_i = jnp.broadcast_to(pw_ref[1, pl.ds(j, 1), lanes], (SUBLANES, width))
                xr, xi = _cmul_add(p_r, p_i, in_r, in_i, hr_ref[pl.ds(r0, SUBLANES), lanes],
                                   hi_ref[pl.ds(r0, SUBLANES), lanes])
                out_r.append(xr)
                out_i.append(xi)
            r16 = pl.multiple_of(jj * 2 * SUBLANES, 2 * SUBLANES)
            hbr_ref[pl.ds(r16, 2 * SUBLANES), lanes] = jnp.concatenate(out_r, axis=0).astype(BF16)
            hbi_ref[pl.ds(r16, 2 * SUBLANES), lanes] = jnp.concatenate(out_i, axis=0).astype(BF16)
            return carry

        lax.fori_loop(0, seg_len // 2, fix_pair, 0, unroll=True)

    _ssm_output(u_ref, hbr_ref, hbi_ref, wcr_ref, wci_ref, d_ref, gw_ref, gb_ref, tile_ref, zg_ref, o_ref,
                1, seg_len)


def _ssm_common_specs(w):
    return [_resident(w["wb_re"].shape), _resident(w["wb_im"].shape),
            _resident(w["wc_re"].shape), _resident(w["wc_im"].shape), _resident(w["a"].shape)]


def _ssm_tail_specs(w):
    return [_resident(w["d"].shape), _resident(w["glu_w"].shape), _resident(w["glu_b"].shape)]


def _ssm_scratch(rows):
    return [pltpu.VMEM((SSM_WIDTH // LANES, rows, LANES), F32), pltpu.VMEM((rows, SSM_WIDTH), F32),
            pltpu.VMEM((rows, SSM_LANES), F32), pltpu.VMEM((rows, SSM_LANES), F32),
            pltpu.VMEM((rows, SSM_LANES), BF16), pltpu.VMEM((rows, SSM_LANES), BF16),
            pltpu.VMEM((rows, SSM_WIDTH), F32)]


def _ssm_prompt(z, w, *, batch, seq, rows):
    nl = seq // rows
    return pl.pallas_call(
        functools.partial(_ssm_prompt_kernel, rows=rows),
        grid=(batch, nl),
        in_specs=([pl.BlockSpec((rows, SSM_WIDTH), lambda b, l: (b * nl + l, 1))] + _ssm_common_specs(w)
                  + [_resident(w["seg"].shape), _resident(w["pw"].shape)] + _ssm_tail_specs(w)),
        out_specs=[pl.BlockSpec((rows, SSM_WIDTH), lambda b, l: (b * nl + l, 0)),
                   pl.BlockSpec((1, 1, SSM_LANES), lambda b, l: (b, 0, 0)),
                   pl.BlockSpec((1, 1, SSM_LANES), lambda b, l: (b, 0, 0))],
        out_shape=[jax.ShapeDtypeStruct((batch * seq, SSM_WIDTH), BF16),
                   jax.ShapeDtypeStruct((batch, 1, SSM_LANES), F32),
                   jax.ShapeDtypeStruct((batch, 1, SSM_LANES), F32)],
        scratch_shapes=_ssm_scratch(rows) + [pltpu.VMEM((2, SUBLANES, SSM_LANES), F32)],
        compiler_params=_params("parallel", "arbitrary"),
        name="ssm_prompt",
    )(z, w["wb_re"], w["wb_im"], w["wc_re"], w["wc_im"], w["a"], w["seg"], w["pw"], w["d"], w["glu_w"], w["glu_b"])


def _ssm_sample_kernel(u_ref, h0r_ref, h0i_ref, wbr_ref, wbi_ref, wcr_ref, wci_ref, a_ref, d_ref, gw_ref, gb_ref,
                       o_ref, sr_ref, si_ref, tile_ref, up_ref, hr_ref, hi_ref, hbr_ref, hbi_ref, zg_ref, *, bb, seq):
    groups = bb // SUBLANES
    width = SCAN_LANE_BLOCK
    _gather_rows(u_ref, tile_ref, up_ref, groups, seq)
    _ssm_input_map(up_ref, wbr_ref, wbi_ref, hr_ref, hi_ref)

    for lb in range(SSM_LANES // width):
        lanes = slice(lb * width, (lb + 1) * width)
        a_r, a_i = a_ref[0, :, lanes], a_ref[1, :, lanes]
        for g in range(groups):
            seqs = slice(g * SUBLANES, (g + 1) * SUBLANES)
            h_r, h_i = h0r_ref[seqs, lanes], h0i_ref[seqs, lanes]
            for jj in range(seq // 2):
                out_r, out_i = [], []
                for d in range(2):
                    r0 = (g * seq + 2 * jj + d) * SUBLANES
                    h_r, h_i = _cmul_add(a_r, a_i, h_r, h_i, hr_ref[r0:r0 + SUBLANES, lanes],
                                         hi_ref[r0:r0 + SUBLANES, lanes])
                    out_r.append(h_r)
                    out_i.append(h_i)
                r16 = (g * seq + 2 * jj) * SUBLANES
                hbr_ref[r16:r16 + 2 * SUBLANES, lanes] = jnp.concatenate(out_r, axis=0).astype(BF16)
                hbi_ref[r16:r16 + 2 * SUBLANES, lanes] = jnp.concatenate(out_i, axis=0).astype(BF16)
            sr_ref[seqs, lanes] = h_r
            si_ref[seqs, lanes] = h_i

    _ssm_output(u_ref, hbr_ref, hbi_ref, wcr_ref, wci_ref, d_ref, gw_ref, gb_ref, tile_ref, zg_ref, o_ref,
                groups, seq)


def _ssm_sample(z, h_re, h_im, w, *, batch, seq, bb):
    assert seq % 2 == 0 and bb % SUBLANES == 0
    rows = bb * seq
    return pl.pallas_call(
        functools.partial(_ssm_sample_kernel, bb=bb, seq=seq),
        grid=(batch // bb,),
        in_specs=([pl.BlockSpec((rows, SSM_WIDTH), lambda i: (i, 1)),
                   pl.BlockSpec((bb, SSM_LANES), lambda i: (i, 0)),
                   pl.BlockSpec((bb, SSM_LANES), lambda i: (i, 0))] + _ssm_common_specs(w) + _ssm_tail_specs(w)),
        out_specs=[pl.BlockSpec((rows, SSM_WIDTH), lambda i: (i, 0)),
                   pl.BlockSpec((bb, SSM_LANES), lambda i: (i, 0)),
                   pl.BlockSpec((bb, SSM_LANES), lambda i: (i, 0))],
        out_shape=[jax.ShapeDtypeStruct((batch * seq, SSM_WIDTH), BF16),
                   jax.ShapeDtypeStruct((batch, SSM_LANES), F32),
                   jax.ShapeDtypeStruct((batch, SSM_LANES), F32)],
        scratch_shapes=_ssm_scratch(rows),
        compiler_params=_params("parallel"),
        name="ssm_sample",
    )(z, h_re, h_im, w["wb_re"], w["wb_im"], w["wc_re"], w["wc_im"], w["a"], w["d"], w["glu_w"], w["glu_b"])


def _ssm_weights(a_re, a_im, log_dt, b_re, b_im, c_re, c_im, d_skip, glu_w, glu_b, *, seg_len):
    dt = jnp.exp(log_dt)[:, None]
    mag = jnp.exp(dt * a_re)
    abr, abi = mag * jnp.cos(dt * a_im), mag * jnp.sin(dt * a_im)
    den = a_re * a_re + a_im * a_im
    xr, xi = abr - 1.0, abi
    fr = (xr * a_re + xi * a_im) / den
    fi = (xi * a_re - xr * a_im) / den
    bbr = fr[..., None] * b_re - fi[..., None] * b_im
    bbi = fr[..., None] * b_im + fi[..., None] * b_re
    eye = jnp.eye(SSM_CHUNK_GROUPS, dtype=F32)

    def in_map(bb):
        t = bb.transpose(0, 2, 1).reshape(SSM_CHUNKS, SSM_CHUNK_GROUPS, SSM_GROUP_IN, SSM_STATE)
        t = t[:, :, :, None, :] * eye[None, :, None, :, None]
        return t.reshape(SSM_CHUNKS, SSM_CHUNK_IN, SSM_CHUNK_LANES).astype(BF16)

    def out_map(cc):
        t = cc.reshape(SSM_CHUNKS, SSM_CHUNK_GROUPS, SSM_GROUP_IN, SSM_STATE).transpose(0, 1, 3, 2)
        t = t[:, :, :, None, :] * eye[None, :, None, :, None]
        return t.reshape(SSM_CHUNKS, SSM_CHUNK_LANES, SSM_CHUNK_IN).astype(BF16)

    def powers(base, n):
        tr, ti = base
        while tr.shape[0] < n:
            qr, qi = tr[-1:], ti[-1:]
            tr, ti = (jnp.concatenate([tr, tr * qr - ti * qi], axis=0),
                      jnp.concatenate([ti, tr * qi + ti * qr], axis=0))
        return tr, ti

    assert seg_len & (seg_len - 1) == 0
    a_row = (abr.reshape(1, SSM_LANES), abi.reshape(1, SSM_LANES))
    ones = jnp.ones((SUBLANES, 1), F32)
    a_b = jnp.stack([ones * a_row[0], ones * a_row[1]])
    pw_r, pw_i = powers(a_row, seg_len)
    sp_r, sp_i = powers((pw_r[-1:], pw_i[-1:]), SUBLANES)
    row = jnp.arange(SUBLANES)[:, None]
    seg = []
    for s in SCAN_STEPS:
        seg += [jnp.where(row >= s, sp_r[s - 1:s], 0.0), jnp.where(row >= s, sp_i[s - 1:s], 0.0)]
    seg += [sp_r, sp_i]
    pw = jnp.stack([pw_r, pw_i])
    return {"wb_re": in_map(bbr), "wb_im": in_map(bbi), "wc_re": out_map(c_re), "wc_im": out_map(c_im),
            "a": a_b, "seg": jnp.stack(seg), "pw": pw, "d": d_skip.reshape(1, SSM_WIDTH),
            "glu_w": glu_w.astype(BF16), "glu_b": glu_b.reshape(1, SSM_WIDTH)}


def _attend(q, k, v):
    s = lax.dot_general(q, k, (((1,), (1,)), ((), ())), preferred_element_type=F32)
    s = s * (XA_HEAD_DIM ** -0.5)
    p = jnp.exp(s - jnp.max(s, axis=-1, keepdims=True))
    p = p / jnp.sum(p, axis=-1, keepdims=True)
    return _dot(p.astype(BF16), v)


def _xa_prompt_kernel(q_ref, k_ref, v_ref, o_ref):
    for h in range(XA_HEADS):
        cols = slice(h * XA_HEAD_DIM, (h + 1) * XA_HEAD_DIM)
        o = _attend(q_ref[:, cols], k_ref[:, cols].astype(BF16), v_ref[:, cols].astype(BF16))
        o_ref[:, cols] = o.astype(BF16)


def _xa_prompt(z, kv, *, batch, seq, rows):
    nl = seq // rows
    return pl.pallas_call(
        _xa_prompt_kernel,
        grid=(batch, nl),
        in_specs=[pl.BlockSpec((rows, XA_WIDTH), lambda b, l: (b * nl + l, 0)),
                  pl.BlockSpec((N_MEM, XA_WIDTH), lambda b, l: (b, 0)),
                  pl.BlockSpec((N_MEM, XA_WIDTH), lambda b, l: (b, 1))],
        out_specs=pl.BlockSpec((rows, XA_WIDTH), lambda b, l: (b * nl + l, 0)),
        out_shape=jax.ShapeDtypeStruct((batch * seq, XA_WIDTH), BF16),
        compiler_params=_params("parallel", "arbitrary"),
        name="xattn_prompt",
    )(z, kv, kv)


def _kv_rows(cache):
    b = cache.shape[0]
    c = cache.reshape(b, N_MEM, XA_HEADS, XA_HALVES, LANES).transpose(0, 1, 3, 2, 4)
    return c.reshape(b, N_MEM * XA_HALVES * XA_HEADS, LANES)


def _xa_sample_kernel(q_ref, k_ref, v_ref, o_ref, *, bb, seq):
    hl = XA_HEADS * seq
    n_rows = N_MEM * XA_HALVES * XA_HEADS
    col = lax.broadcasted_iota(jnp.int32, (hl, n_rows), 1)
    row = lax.broadcasted_iota(jnp.int32, (hl, n_rows), 0)
    slot = col & (XA_HALVES * XA_HEADS - 1)
    head = row // seq
    first_half = slot == head
    second_half = slot == head + XA_HEADS
    q_all = q_ref[...].astype(F32)
    for b in range(bb):
        rows = slice(b * seq, (b + 1) * seq)
        q = q_all[rows, :]
        qm = jnp.concatenate([q[:, h * XA_HEAD_DIM + t * LANES:h * XA_HEAD_DIM + (t + 1) * LANES]
                              for t in range(XA_HALVES) for h in range(XA_HEADS)], axis=0)
        g = lax.dot_general(qm.astype(BF16), k_ref[b].astype(BF16), (((1,), (1,)), ((), ())),
                            preferred_element_type=F32)
        g0 = jnp.where(first_half, g[:hl], 0.0)
        g1 = jnp.where(second_half, g[hl:], 0.0)
        s = (g0 + pltpu.roll(g1, n_rows - XA_HEADS, 1)) * (XA_HEAD_DIM ** -0.5)
        s = jnp.where(first_half, s, -jnp.inf)
        p = jnp.exp(s - jnp.max(s, axis=-1, keepdims=True))
        p = p / jnp.sum(p, axis=-1, keepdims=True)
        pp = jnp.concatenate([p, pltpu.roll(p, XA_HEADS, 1)], axis=0).astype(BF16)
        o = _dot(pp, v_ref[b].astype(BF16))
        for t in range(XA_HALVES):
            for h in range(XA_HEADS):
                r0 = t * hl + h * seq
                c0 = h * XA_HEAD_DIM + t * LANES
                o_ref[rows, c0:c0 + LANES] = o[r0:r0 + seq].astype(BF16)


def _xa_sample(z, mem_k, mem_v, *, batch, seq, bb):
    n_rows = N_MEM * XA_HALVES * XA_HEADS
    return pl.pallas_call(
        functools.partial(_xa_sample_kernel, bb=bb, seq=seq),
        grid=(batch // bb,),
        in_specs=[pl.BlockSpec((bb * seq, XA_WIDTH), lambda i: (i, 0)),
                  pl.BlockSpec((bb, n_rows, LANES), lambda i: (i, 0, 0)),
                  pl.BlockSpec((bb, n_rows, LANES), lambda i: (i, 0, 0))],
        out_specs=pl.BlockSpec((bb * seq, XA_WIDTH), lambda i: (i, 0)),
        out_shape=jax.ShapeDtypeStruct((batch * seq, XA_WIDTH), BF16),
        compiler_params=_params("parallel"),
        name="xattn_sample",
    )(z, mem_k, mem_v)


def _merge_kernel(ap_ref, as_ref, ax_ref, g0_ref, g1_ref, g2_ref, x_ref, wp_ref, ws_ref, wx_ref, wo_ref, o_ref):
    merged = (g0_ref[...] * _dot(ap_ref[...], wp_ref[...])
              + g1_ref[...] * _dot(as_ref[...], ws_ref[...])
              + g2_ref[...] * _dot(ax_ref[...], wx_ref[...]))
    o_ref[...] = x_ref[...] + _dot(merged.astype(BF16), wo_ref[...])


def _merge(a_pool, a_ssm, a_xa, gates, x, wp, ws, wx, wo, *, tm):
    m, d = x.shape
    act = pl.BlockSpec((tm, a_pool.shape[1]), lambda i: (i, 0))
    return pl.pallas_call(
        _merge_kernel,
        grid=(m // tm,),
        in_specs=[act, act, act,
                  pl.BlockSpec((tm, d), lambda i: (i, 0)),
                  pl.BlockSpec((tm, d), lambda i: (i, 1)),
                  pl.BlockSpec((tm, d), lambda i: (i, 2)),
                  pl.BlockSpec((tm, d), lambda i: (i, 0)),
                  _resident(wp.shape), _resident(ws.shape), _resident(wx.shape), _resident(wo.shape)],
        out_specs=pl.BlockSpec((tm, d), lambda i: (i, 0)),
        out_shape=jax.ShapeDtypeStruct((m, d), F32),
        compiler_params=_params("parallel"),
        name="merge",
    )(a_pool, a_ssm, a_xa, gates, gates, gates, x, wp, ws, wx, wo)


def _mlp_kernel(x_ref, g_ref, w1_ref, w2_ref, gf_ref, o_ref, xn_ref):
    f = pl.program_id(1)

    @pl.when(f == 0)
    def _():
        _norm_to_scratch(x_ref, g_ref, xn_ref)
        o_ref[...] = x_ref[...]

    h = jnp.square(jnp.maximum(_dot(xn_ref[...], w1_ref[...]), 0.0))
    o_ref[...] += _dot(h.astype(BF16), w2_ref[...])

    @pl.when(f == pl.num_programs(1) - 1)
    def _():
        y = o_ref[...]
        o_ref[...] = (y * _rms_scale(y)) * gf_ref[...]


def _mlp(x, g, w1, w2, gf, *, tm, tf):
    m, d = x.shape
    dff = w1.shape[1]
    return pl.pallas_call(
        _mlp_kernel,
        grid=(m // tm, dff // tf),
        in_specs=[pl.BlockSpec((tm, d), lambda i, f: (i, 0)),
                  pl.BlockSpec((1, d), lambda i, f: (0, 0)),
                  pl.BlockSpec((d, tf), lambda i, f: (0, f)),
                  pl.BlockSpec((tf, d), lambda i, f: (f, 0)),
                  pl.BlockSpec((1, d), lambda i, f: (0, 0))],
        out_specs=pl.BlockSpec((tm, d), lambda i, f: (i, 0)),
        out_shape=jax.ShapeDtypeStruct((m, d), F32),
        scratch_shapes=[pltpu.VMEM((tm, d), BF16)],
        compiler_params=_params("parallel", "arbitrary"),
        name="mlp",
    )(x, g.reshape(1, d), w1, w2, gf.reshape(1, d))


def _layer_weights(l, norm1_g, w_in, b_gate, pool_w, pool_scale, pool_proj, ssm_A_re, ssm_A_im, ssm_log_dt,
                   ssm_B_re, ssm_B_im, ssm_C_re, ssm_C_im, ssm_D, ssm_glu_w, ssm_glu_b, ssm_proj,
                   mem_norm_g, xa_wk, xa_wv, xa_wo, w_out, norm2_g, mlp_w1, mlp_w2):
    return {
        "norm1_g": norm1_g[l], "w_in": w_in[l].astype(BF16), "b_gate": b_gate[l].reshape(-1),
        "pool_w": pool_w[l].astype(BF16), "pool_scale": pool_scale[l].reshape(1, POOL_WIDTH),
        "pool_proj": pool_proj[l].astype(BF16),
        "ssm": _ssm_weights(ssm_A_re[l], ssm_A_im[l], ssm_log_dt[l], ssm_B_re[l], ssm_B_im[l],
                            ssm_C_re[l], ssm_C_im[l], ssm_D[l], ssm_glu_w[l], ssm_glu_b[l],
                            seg_len=SSM_PROMPT_ROWS // SUBLANES),
        "ssm_proj": ssm_proj[l].astype(BF16),
        "mem_norm_g": mem_norm_g[l],
        "w_kv": jnp.concatenate([xa_wk[l], xa_wv[l]], axis=1).astype(BF16),
        "xa_wo": xa_wo[l].astype(BF16), "w_out": w_out[l].astype(BF16),
        "norm2_g": norm2_g[l], "mlp_w1": mlp_w1[l].astype(BF16), "mlp_w2": mlp_w2[l].astype(BF16),
    }


def _finish_layer(x, gates, a_pool, a_ssm, a_xa, w, final_g):
    x = _merge(a_pool, a_ssm, a_xa, gates, x, w["pool_proj"], w["ssm_proj"], w["xa_wo"], w["w_out"], tm=MERGE_TM)
    return _mlp(x, w["norm2_g"], w["mlp_w1"], w["mlp_w2"], final_g, tm=MLP_TM, tf=MLP_TF)


def kernel(x_prompt, x_sample, state_pool, state_ssm_re, state_ssm_im, cache_mem_k, cache_mem_v, mem_prompt,
           norm1_g, w_in, b_gate, pool_w, pool_scale, pool_proj, ssm_A_re, ssm_A_im, ssm_log_dt, ssm_B_re,
           ssm_B_im, ssm_C_re, ssm_C_im, ssm_D, ssm_glu_w, ssm_glu_b, ssm_proj, mem_norm_g, xa_wk, xa_wv,
           xa_wo, w_out, norm2_g, mlp_w1, mlp_w2, final_norm_g):
    depth = norm1_g.shape[0]
    assert depth == 1, "the final norm is fused into the single layer's MLP kernel"
    bp, lp, d = x_prompt.shape
    bs, ls, _ = x_sample.shape
    n_prev_sample = min(POOL_BUF, PAST_LEN)
    w = _layer_weights(0, norm1_g, w_in, b_gate, pool_w, pool_scale, pool_proj, ssm_A_re, ssm_A_im, ssm_log_dt,
                       ssm_B_re, ssm_B_im, ssm_C_re, ssm_C_im, ssm_D, ssm_glu_w, ssm_glu_b, ssm_proj,
                       mem_norm_g, xa_wk, xa_wv, xa_wo, w_out, norm2_g, mlp_w1, mlp_w2)

    xp = x_prompt.reshape(bp * lp, d)
    kv = _norm_matmul(mem_prompt.reshape(bp * N_MEM, d), w["mem_norm_g"], w["w_kv"], tm=IN_TM, tn=IN_TN,
                      name="memory_kv")
    up, qp, gp = _in_proj(xp, w["norm1_g"], w["w_in"], w["b_gate"], tm=IN_TM, name="in_proj_prompt")
    a_pool, pool_p = _pool_prompt(up, w["pool_w"], w["pool_scale"], batch=bp, seq=lp, rows=POOL_PROMPT_ROWS)
    a_ssm, re_p, im_p = _ssm_prompt(up, w["ssm"], batch=bp, seq=lp, rows=SSM_PROMPT_ROWS)
    a_xa = _xa_prompt(qp, kv, batch=bp, seq=lp, rows=XA_PROMPT_ROWS)
    y_prompt = _finish_layer(xp, gp, a_pool, a_ssm, a_xa, w, final_norm_g).reshape(bp, lp, d)

    xs = x_sample.reshape(bs * ls, d)
    us, qs, gs = _in_proj(xs, w["norm1_g"], w["w_in"], w["b_gate"], tm=IN_TM, name="in_proj_sample")
    a_pool, pool_s = _pool_sample(us, state_pool[0], w["pool_w"], w["pool_scale"], batch=bs, seq=ls,
                                  bb=POOL_SAMPLE_SEQS, n_prev=n_prev_sample)
    a_ssm, re_s, im_s = _ssm_sample(us, state_ssm_re[0].reshape(bs, SSM_LANES),
                                    state_ssm_im[0].reshape(bs, SSM_LANES), w["ssm"], batch=bs, seq=ls,
                                    bb=SSM_SAMPLE_SEQS)
    a_xa = _xa_sample(qs, _kv_rows(cache_mem_k[0]), _kv_rows(cache_mem_v[0]), batch=bs, seq=ls, bb=XA_SAMPLE_SEQS)
    y_sample = _finish_layer(xs, gs, a_pool, a_ssm, a_xa, w, final_norm_g).reshape(bs, ls, d)

    state_shape = (1, -1, SSM_GROUPS, SSM_STATE)
    mem_shape = (1, bp, N_MEM, XA_HEADS, XA_HEAD_DIM)
    return (y_prompt, y_sample,
            pool_p[None], re_p.reshape(state_shape), im_p.reshape(state_shape),
            kv[:, :XA_WIDTH].reshape(mem_shape), kv[:, XA_WIDTH:].reshape(mem_shape),
            pool_s[None], re_s.reshape(state_shape), im_s.reshape(state_shape))
```

```python
import functools
import math

import jax
import jax.numpy as jnp
from jax import lax
from jax.experimental import pallas as pl
from jax.experimental.pallas import tpu as pltpu

F32 = jnp.float32
BF16 = jnp.bfloat16

D_MODEL = 2048
POOL_WINDOWS = (2, 4, 8, 16)
POOL_GROUP_W = 256
POOL_WIDTH = 1024
POOL_BUF = 15
SSM_WIDTH = 1024
SSM_GROUPS = 64
SSM_GROUP_IN = 16
SSM_STATE = 64
SSM_LANES = SSM_GROUPS * SSM_STATE
SSM_CHUNKS = 4
SSM_CHUNK_GROUPS = SSM_GROUPS // SSM_CHUNKS
SSM_CHUNK_IN = SSM_WIDTH // SSM_CHUNKS
SSM_CHUNK_LANES = SSM_LANES // SSM_CHUNKS
XA_HEADS = 4
XA_HEAD_DIM = 256
XA_WIDTH = 1024
N_MEM = 256
N_BRANCH = 3
D_FF = 4 * D_MODEL
EPS = 1e-6
PAST_LEN = 16384

SUBLANES = 8
LANES = 128
XA_HALVES = XA_HEAD_DIM // LANES
POOL_HIST = 2 * SUBLANES
SCAN_STEPS = (1, 2, 4)
SCAN_LANE_BLOCK = 512
SSM_PROMPT_ROWS = 256

IN_TM = 1024
IN_TN = 1024
IN_SAMPLE_TN = 512
MERGE_TM = 256
MLP_TM = 1024
MLP_TF = 512
MLP_SAMPLE_TF = 256
POOL_PROMPT_ROWS = 512
XA_PROMPT_ROWS = 512
POOL_SAMPLE_SEQS = 16
SSM_SAMPLE_SEQS = 32
XA_SAMPLE_SEQS = 4
VMEM_LIMIT = 56 * 1024 * 1024


def _params(*sem):
    return pltpu.CompilerParams(dimension_semantics=sem, vmem_limit_bytes=VMEM_LIMIT)


def _resident(shape):
    nd = len(shape)
    return pl.BlockSpec(shape, lambda *_: (0,) * nd, pipeline_mode=pl.Buffered(1))


def _rms_scale(x):
    return lax.rsqrt(jnp.mean(x * x, axis=-1, keepdims=True) + EPS)


def _dot(a, b):
    return jnp.dot(a, b, preferred_element_type=F32)


def _norm_to_scratch(x_ref, g_ref, xn_ref):
    x = x_ref[...]
    xn_ref[...] = ((x * _rms_scale(x)) * g_ref[...]).astype(BF16)


def _in_proj_kernel(x_ref, g_ref, w_ref, b_ref, u_ref, q_ref, gate_ref, *rest, u_blocks, gate0, emit_w):
    if emit_w:
        wb_ref, xn_ref = rest
    else:
        (xn_ref,) = rest
    j = pl.program_id(1)

    @pl.when(j == 0)
    def _():
        _norm_to_scratch(x_ref, g_ref, xn_ref)

    if emit_w:
        wb_ref[...] = w_ref[...].astype(BF16)

    def w_block():
        return wb_ref[...] if emit_w else w_ref[...]

    @pl.when(j < u_blocks)
    def _():
        u_ref[...] = _dot(xn_ref[...], w_block())

    @pl.when(jnp.logical_and(j >= u_blocks, j < gate0))
    def _():
        q_ref[...] = _dot(xn_ref[...], w_block()).astype(BF16)

    @pl.when(j >= gate0)
    def _():
        gate_ref[...] = jax.nn.sigmoid(_dot(xn_ref[...], w_block()) + b_ref[...]).astype(BF16)


def _in_proj(x, g, w_in, b_gate, *, tm, tn, name):
    m, d = x.shape
    n = w_in.shape[1]
    emit_w = w_in.dtype != BF16
    assert not emit_w or m == tm, "the bf16 weight copy is written once, by a single row tile"
    u_blocks = (POOL_WIDTH + SSM_WIDTH) // tn
    q_blocks = XA_WIDTH // tn
    gate0 = u_blocks + q_blocks
    n_gate = n // tn - gate0
    out_specs = [pl.BlockSpec((tm, tn), lambda i, j: (i, jnp.minimum(j, u_blocks - 1))),
                 pl.BlockSpec((tm, tn), lambda i, j: (i, jnp.clip(j - u_blocks, 0, q_blocks - 1))),
                 pl.BlockSpec((tm, tn), lambda i, j: (i, jnp.maximum(j - gate0, 0)))]
    out_shape = [jax.ShapeDtypeStruct((m, u_blocks * tn), F32),
                 jax.ShapeDtypeStruct((m, q_blocks * tn), BF16),
                 jax.ShapeDtypeStruct((m, n_gate * tn), BF16)]
    if emit_w:
        out_specs.append(pl.BlockSpec((d, tn), lambda i, j: (0, j)))
        out_shape.append(jax.ShapeDtypeStruct((d, n), BF16))
    return pl.pallas_call(
        functools.partial(_in_proj_kernel, u_blocks=u_blocks, gate0=gate0, emit_w=emit_w),
        grid=(m // tm, n // tn),
        in_specs=[pl.BlockSpec((tm, d), lambda i, j: (i, 0)),
                  pl.BlockSpec((1, d), lambda i, j: (0, 0)),
                  pl.BlockSpec((d, tn), lambda i, j: (0, j)),
                  pl.BlockSpec((1, tn), lambda i, j: (0, jnp.maximum(j - gate0, 0)))],
        out_specs=out_specs,
        out_shape=out_shape,
        scratch_shapes=[pltpu.VMEM((tm, d), BF16)],
        compiler_params=_params("parallel", "arbitrary"),
        name=name,
    )(x, g.reshape(1, d), w_in, b_gate.reshape(1, n_gate * tn))


def _window_means(e_ref, row0, rows, first_pos):
    outs = []
    for k, wdw in enumerate(POOL_WINDOWS):
        lo = k * POOL_GROUP_W
        cur = e_ref[row0:row0 + rows, lo:lo + POOL_GROUP_W]
        s = cur
        for d in range(1, wdw):
            s = s + e_ref[row0 - d:row0 - d + rows, lo:lo + POOL_GROUP_W]
        pos = lax.broadcasted_iota(jnp.int32, (rows, POOL_GROUP_W), 0) + first_pos
        cnt = jnp.minimum(pos, wdw).astype(F32)
        outs.append(s / cnt - cur)
    return outs


def _pool_mix(pooled, k, pw_ref, ps_ref):
    lo = k * POOL_GROUP_W
    mixed = _dot(pooled.astype(BF16), pw_ref[k])
    return (mixed * ps_ref[:, lo:lo + POOL_GROUP_W]).astype(BF16)


def _pool_prompt_kernel(u_ref, pw_ref, ps_ref, o_ref, st_ref, e_ref, *, rows):
    l = pl.program_id(1)

    @pl.when(l == 0)
    def _():
        e_ref[0:POOL_HIST, :] = jnp.zeros((POOL_HIST, POOL_WIDTH), F32)

    @pl.when(l > 0)
    def _():
        e_ref[0:POOL_HIST, :] = e_ref[rows:rows + POOL_HIST, :]

    e_ref[POOL_HIST:POOL_HIST + rows, :] = u_ref[...]
    pooled = _window_means(e_ref, POOL_HIST, rows, l * rows + 1)
    for k, p in enumerate(pooled):
        o_ref[:, k * POOL_GROUP_W:(k + 1) * POOL_GROUP_W] = _pool_mix(p, k, pw_ref, ps_ref)

    @pl.when(l == pl.num_programs(1) - 1)
    def _():
        st_ref[0] = e_ref[POOL_HIST + rows - POOL_BUF:POOL_HIST + rows, :]


def _pool_prompt(z, pool_w, pool_scale, *, batch, seq, rows):
    nl = seq // rows
    return pl.pallas_call(
        functools.partial(_pool_prompt_kernel, rows=rows),
        grid=(batch, nl),
        in_specs=[pl.BlockSpec((rows, POOL_WIDTH), lambda b, l: (b * nl + l, 0)),
                  _resident(pool_w.shape), _resident(pool_scale.shape)],
        out_specs=[pl.BlockSpec((rows, POOL_WIDTH), lambda b, l: (b * nl + l, 0)),
                   pl.BlockSpec((None, 1, POOL_BUF, POOL_WIDTH), lambda b, l: (0, b, 0, 0))],
        out_shape=[jax.ShapeDtypeStruct((batch * seq, POOL_WIDTH), BF16),
                   jax.ShapeDtypeStruct((1, batch, POOL_BUF, POOL_WIDTH), F32)],
        scratch_shapes=[pltpu.VMEM((rows + POOL_HIST, POOL_WIDTH), F32)],
        compiler_params=_params("parallel", "arbitrary"),
        name="pool_prompt",
    )(z, pool_w, pool_scale)


def _pool_sample_kernel(u_ref, prev_ref, pw_ref, ps_ref, o_ref, st_ref, e_ref, p_ref, *, bb, seq, n_prev):
    span = POOL_HIST + seq
    for b in range(bb):
        top = b * span + POOL_HIST
        e_ref[top - POOL_BUF:top, :] = prev_ref[b]
        e_ref[top:top + seq, :] = u_ref[b * seq:(b + 1) * seq, :]
    for b in range(bb):
        top = b * span + POOL_HIST
        st_ref[b] = e_ref[top + seq - POOL_BUF:top + seq, :]
        pooled = _window_means(e_ref, top, seq, n_prev + 1)
        for k, p in enumerate(pooled):
            p_ref[b * seq:(b + 1) * seq, k * POOL_GROUP_W:(k + 1) * POOL_GROUP_W] = p
    for k in range(len(POOL_WINDOWS)):
        lo = k * POOL_GROUP_W
        o_ref[:, lo:lo + POOL_GROUP_W] = _pool_mix(p_ref[:, lo:lo + POOL_GROUP_W], k, pw_ref, ps_ref)


def _pool_sample(z, prev, pool_w, pool_scale, *, batch, seq, bb, n_prev):
    return pl.pallas_call(
        functools.partial(_pool_sample_kernel, bb=bb, seq=seq, n_prev=n_prev),
        grid=(batch // bb,),
        in_specs=[pl.BlockSpec((bb * seq, POOL_WIDTH), lambda i: (i, 0)),
                  pl.BlockSpec((None, bb, POOL_BUF, POOL_WIDTH), lambda i: (0, i, 0, 0)),
                  _resident(pool_w.shape), _resident(pool_scale.shape)],
        out_specs=[pl.BlockSpec((bb * seq, POOL_WIDTH), lambda i: (i, 0)),
                   pl.BlockSpec((None, bb, POOL_BUF, POOL_WIDTH), lambda i: (0, i, 0, 0))],
        out_shape=[jax.ShapeDtypeStruct((batch * seq, POOL_WIDTH), BF16),
                   jax.ShapeDtypeStruct((1, batch, POOL_BUF, POOL_WIDTH), F32)],
        scratch_shapes=[pltpu.VMEM((bb * (POOL_HIST + seq), POOL_WIDTH), F32),
                        pltpu.VMEM((bb * seq, POOL_WIDTH), F32)],
        compiler_params=_params("parallel"),
        name="pool_sample",
    )(z, prev, pool_w, pool_scale)


def _gather_rows(src_ref, tile_ref, dst_ref, groups, seg_len):
    n_tiles = src_ref.shape[1] // LANES
    for c in range(n_tiles):
        tile_ref[c] = src_ref[:, c * LANES:(c + 1) * LANES]
    for g in range(groups):
        for j in range(seg_len):
            r = (g * seg_len + j) * SUBLANES
            for c in range(n_tiles):
                dst_ref[r:r + SUBLANES, c * LANES:(c + 1) * LANES] = (
                    tile_ref[c, pl.ds(g * SUBLANES * seg_len + j, SUBLANES, stride=seg_len), :])


def _scatter_rows(y, tile_ref, tile0, groups, seg_len):
    for g in range(groups):
        for j in range(seg_len):
            r = (g * seg_len + j) * SUBLANES
            for c in range(y.shape[1] // LANES):
                tile_ref[tile0 + c, pl.ds(g * SUBLANES * seg_len + j, SUBLANES, stride=seg_len), :] = (
                    y[r:r + SUBLANES, c * LANES:(c + 1) * LANES])


def _ssm_input_map(up_ref, wbr_ref, wbi_ref, hr_ref, hi_ref):
    for c in range(SSM_CHUNKS):
        uc = up_ref[:, c * SSM_CHUNK_IN:(c + 1) * SSM_CHUNK_IN].astype(BF16)
        lanes = slice(c * SSM_CHUNK_LANES, (c + 1) * SSM_CHUNK_LANES)
        hr_ref[:, lanes] = _dot(uc, wbr_ref[c])
        hi_ref[:, lanes] = _dot(uc, wbi_ref[c])


def _cmul_add(ar, ai, hr, hi, xr, xi):
    return xr + (ar * hr - ai * hi), xi + (ar * hi + ai * hr)


def _scan_rows(xr, xi, coef, car_r, car_i):
    for n, s in enumerate(SCAN_STEPS):
        xr, xi = _cmul_add(coef[2 * n], coef[2 * n + 1], pltpu.roll(xr, s, 0), pltpu.roll(xi, s, 0), xr, xi)
    return _cmul_add(coef[2 * len(SCAN_STEPS)], coef[2 * len(SCAN_STEPS) + 1], car_r, car_i, xr, xi)


def _last_row(x):
    return jnp.broadcast_to(x[SUBLANES - 1:SUBLANES, :], x.shape)


def _ssm_output(u_ref, hbr_ref, hbi_ref, wcr_ref, wci_ref, d_ref, gw_ref, gb_ref, tile_ref, zg_ref, o_ref,
                groups, seg_len):
    for c in range(SSM_CHUNKS):
        lanes = slice(c * SSM_CHUNK_LANES, (c + 1) * SSM_CHUNK_LANES)
        y = _dot(hbr_ref[:, lanes], wcr_ref[c]) - _dot(hbi_ref[:, lanes], wci_ref[c])
        _scatter_rows(y, tile_ref, c * (SSM_CHUNK_IN // LANES), groups, seg_len)
    for c in range(SSM_WIDTH // LANES):
        cols = slice(c * LANES, (c + 1) * LANES)
        zg_ref[:, cols] = jax.nn.gelu(tile_ref[c] + d_ref[:, cols] * u_ref[:, cols])
    zg = zg_ref[...]
    gate = jax.nn.sigmoid(_dot(zg.astype(BF16), gw_ref[...]) + gb_ref[...])
    o_ref[...] = (zg * gate).astype(BF16)


def _ssm_prompt_kernel(u_ref, wbr_ref, wbi_ref, wcr_ref, wci_ref, a_ref, seg_ref, pw_ref, d_ref, gw_ref, gb_ref,
                       o_ref, sr_ref, si_ref, tile_ref, up_ref, hr_ref, hi_ref, hbr_ref, hbi_ref, zg_ref, car_ref,
                       *, rows):
    seg_len = rows // SUBLANES
    width = SCAN_LANE_BLOCK

    @pl.when(pl.program_id(1) == 0)
    def _():
        car_ref[...] = jnp.zeros(car_ref.shape, F32)

    _gather_rows(u_ref, tile_ref, up_ref, 1, seg_len)
    _ssm_input_map(up_ref, wbr_ref, wbi_ref, hr_ref, hi_ref)

    for lb in range(SSM_LANES // width):
        lanes = slice(lb * width, (lb + 1) * width)
        a_r, a_i = a_ref[0, :, lanes], a_ref[1, :, lanes]

        def local_step(j, h):
            r0 = pl.multiple_of(j * SUBLANES, SUBLANES)
            nr, ni = _cmul_add(a_r, a_i, h[0], h[1], hr_ref[pl.ds(r0, SUBLANES), lanes],
                               hi_ref[pl.ds(r0, SUBLANES), lanes])
            hr_ref[pl.ds(r0, SUBLANES), lanes] = nr
            hi_ref[pl.ds(r0, SUBLANES), lanes] = ni
            return nr, ni

        zero = jnp.zeros((SUBLANES, width), F32)
        loc_r, loc_i = lax.fori_loop(0, seg_len, local_step, (zero, zero), unroll=True)

        car_r, car_i = car_ref[0, :, lanes], car_ref[1, :, lanes]
        seg = tuple(seg_ref[n, :, lanes] for n in range(2 * len(SCAN_STEPS) + 2))
        end_r, end_i = _scan_rows(loc_r, loc_i, seg, car_r, car_i)
        car_ref[0, :, lanes] = _last_row(end_r)
        car_ref[1, :, lanes] = _last_row(end_i)
        sr_ref[0, :, lanes] = end_r[SUBLANES - 1:SUBLANES, :]
        si_ref[0, :, lanes] = end_i[SUBLANES - 1:SUBLANES, :]
        first = lax.broadcasted_iota(jnp.int32, (SUBLANES, width), 0) == 0
        in_r = jnp.where(first, car_r, pltpu.roll(end_r, 1, 0))
        in_i = jnp.where(first, car_i, pltpu.roll(end_i, 1, 0))

        def fix_pair(jj, carry):
            out_r, out_i = [], []
            for d in range(2):
                j = 2 * jj + d
                r0 = pl.multiple_of(j * SUBLANES, SUBLANES)
                p_r = jnp.broadcast_to(pw_ref[0, pl.ds(j, 1), lanes], (SUBLANES, width))
                p_i = jnp.broadcast_to(pw_ref[1, pl.ds(j, 1), lanes], (SUBLANES, width))
                xr, xi = _cmul_add(p_r, p_i, in_r, in_i, hr_ref[pl.ds(r0, SUBLANES), lanes],
                                   hi_ref[pl.ds(r0, SUBLANES), lanes])
                out_r.append(xr)
                out_i.append(xi)
            r16 = pl.multiple_of(jj * 2 * SUBLANES, 2 * SUBLANES)
            hbr_ref[pl.ds(r16, 2 * SUBLANES), lanes] = jnp.concatenate(out_r, axis=0).astype(BF16)
            hbi_ref[pl.ds(r16, 2 * SUBLANES), lanes] = jnp.concatenate(out_i, axis=0).astype(BF16)
            return carry

        lax.fori_loop(0, seg_len // 2, fix_pair, 0, unroll=True)

    _ssm_output(u_ref, hbr_ref, hbi_ref, wcr_ref, wci_ref, d_ref, gw_ref, gb_ref, tile_ref, zg_ref, o_ref,
                1, seg_len)


def _ssm_common_specs(w):
    return [_resident(w["wb_re"].shape), _resident(w["wb_im"].shape),
            _resident(w["wc_re"].shape), _resident(w["wc_im"].shape), _resident(w["a"].shape)]


def _ssm_tail_specs(w):
    return [_resident(w["d"].shape), _resident(w["glu_w"].shape), _resident(w["glu_b"].shape)]


def _ssm_scratch(rows):
    return [pltpu.VMEM((SSM_WIDTH // LANES, rows, LANES), F32), pltpu.VMEM((rows, SSM_WIDTH), F32),
            pltpu.VMEM((rows, SSM_LANES), F32), pltpu.VMEM((rows, SSM_LANES), F32),
            pltpu.VMEM((rows, SSM_LANES), BF16), pltpu.VMEM((rows, SSM_LANES), BF16),
            pltpu.VMEM((rows, SSM_WIDTH), F32)]


def _ssm_prompt(z, w, *, batch, seq, rows):
    nl = seq // rows
    return pl.pallas_call(
        functools.partial(_ssm_prompt_kernel, rows=rows),
        grid=(batch, nl),
        in_specs=([pl.BlockSpec((rows, SSM_WIDTH), lambda b, l: (b * nl + l, 1))] + _ssm_common_specs(w)
                  + [_resident(w["seg"].shape), _resident(w["pw"].shape)] + _ssm_tail_specs(w)),
        out_specs=[pl.BlockSpec((rows, SSM_WIDTH), lambda b, l: (b * nl + l, 0)),
                   pl.BlockSpec((1, 1, SSM_LANES), lambda b, l: (b, 0, 0)),
                   pl.BlockSpec((1, 1, SSM_LANES), lambda b, l: (b, 0, 0))],
        out_shape=[jax.ShapeDtypeStruct((batch * seq, SSM_WIDTH), BF16),
                   jax.ShapeDtypeStruct((batch, 1, SSM_LANES), F32),
                   jax.ShapeDtypeStruct((batch, 1, SSM_LANES), F32)],
        scratch_shapes=_ssm_scratch(rows) + [pltpu.VMEM((2, SUBLANES, SSM_LANES), F32)],
        compiler_params=_params("parallel", "arbitrary"),
        name="ssm_prompt",
    )(z, w["wb_re"], w["wb_im"], w["wc_re"], w["wc_im"], w["a"], w["seg"], w["pw"], w["d"], w["glu_w"], w["glu_b"])


def _ssm_sample_kernel(u_ref, h0r_ref, h0i_ref, wbr_ref, wbi_ref, wcr_ref, wci_ref, a_ref, d_ref, gw_ref, gb_ref,
                       o_ref, sr_ref, si_ref, tile_ref, up_ref, hr_ref, hi_ref, hbr_ref, hbi_ref, zg_ref, *, bb, seq):
    groups = bb // SUBLANES
    width = SCAN_LANE_BLOCK
    _gather_rows(u_ref, tile_ref, up_ref, groups, seq)
    _ssm_input_map(up_ref, wbr_ref, wbi_ref, hr_ref, hi_ref)

    for lb in range(SSM_LANES // width):
        lanes = slice(lb * width, (lb + 1) * width)
        a_r, a_i = a_ref[0, :, lanes], a_ref[1, :, lanes]
        for g in range(groups):
            seqs = slice(g * SUBLANES, (g + 1) * SUBLANES)
            h_r, h_i = h0r_ref[seqs, lanes], h0i_ref[seqs, lanes]
            for jj in range(seq // 2):
                out_r, out_i = [], []
                for d in range(2):
                    r0 = (g * seq + 2 * jj + d) * SUBLANES
                    h_r, h_i = _cmul_add(a_r, a_i, h_r, h_i, hr_ref[r0:r0 + SUBLANES, lanes],
                                         hi_ref[r0:r0 + SUBLANES, lanes])
                    out_r.append(h_r)
                    out_i.append(h_i)
                r16 = (g * seq + 2 * jj) * SUBLANES
                hbr_ref[r16:r16 + 2 * SUBLANES, lanes] = jnp.concatenate(out_r, axis=0).astype(BF16)
                hbi_ref[r16:r16 + 2 * SUBLANES, lanes] = jnp.concatenate(out_i, axis=0).astype(BF16)
            sr_ref[seqs, lanes] = h_r
            si_ref[seqs, lanes] = h_i

    _ssm_output(u_ref, hbr_ref, hbi_ref, wcr_ref, wci_ref, d_ref, gw_ref, gb_ref, tile_ref, zg_ref, o_ref,
                groups, seq)


def _ssm_sample(z, h_re, h_im, w, *, batch, seq, bb):
    assert seq % 2 == 0 and bb % SUBLANES == 0
    rows = bb * seq
    return pl.pallas_call(
        functools.partial(_ssm_sample_kernel, bb=bb, seq=seq),
        grid=(batch // bb,),
        in_specs=([pl.BlockSpec((rows, SSM_WIDTH), lambda i: (i, 1)),
                   pl.BlockSpec((bb, SSM_LANES), lambda i: (i, 0)),
                   pl.BlockSpec((bb, SSM_LANES), lambda i: (i, 0))] + _ssm_common_specs(w) + _ssm_tail_specs(w)),
        out_specs=[pl.BlockSpec((rows, SSM_WIDTH), lambda i: (i, 0)),
                   pl.BlockSpec((bb, SSM_LANES), lambda i: (i, 0)),
                   pl.BlockSpec((bb, SSM_LANES), lambda i: (i, 0))],
        out_shape=[jax.ShapeDtypeStruct((batch * seq, SSM_WIDTH), BF16),
                   jax.ShapeDtypeStruct((batch, SSM_LANES), F32),
                   jax.ShapeDtypeStruct((batch, SSM_LANES), F32)],
        scratch_shapes=_ssm_scratch(rows),
        compiler_params=_params("parallel"),
        name="ssm_sample",
    )(z, h_re, h_im, w["wb_re"], w["wb_im"], w["wc_re"], w["wc_im"], w["a"], w["d"], w["glu_w"], w["glu_b"])


def _ssm_weights(a_re, a_im, log_dt, b_re, b_im, c_re, c_im, d_skip, glu_w, glu_b, *, seg_len):
    dt = jnp.exp(log_dt)[:, None]
    mag = jnp.exp(dt * a_re)
    abr, abi = mag * jnp.cos(dt * a_im), mag * jnp.sin(dt * a_im)
    den = a_re * a_re + a_im * a_im
    xr, xi = abr - 1.0, abi
    fr = (xr * a_re + xi * a_im) / den
    fi = (xi * a_re - xr * a_im) / den
    bbr = fr[..., None] * b_re - fi[..., None] * b_im
    bbi = fr[..., None] * b_im + fi[..., None] * b_re

    def block_diag(t, width):
        rows = t.shape[1]
        per_group = rows // SSM_CHUNK_GROUPS
        cols = SSM_CHUNK_GROUPS * width
        tile = (jnp.arange(width)[:, None] == jnp.arange(cols)[None, :] % width).astype(BF16)
        tiled = jnp.einsum("crk,kn->crn", t, tile, preferred_element_type=F32)
        same = (jnp.arange(rows)[:, None] // per_group) == (jnp.arange(cols)[None, :] // width)
        return jnp.where(same[None], tiled, 0.0).astype(BF16)

    def in_map(bb):
        t = bb.transpose(0, 2, 1).reshape(SSM_CHUNKS, SSM_CHUNK_IN, SSM_STATE)
        return block_diag(t.astype(BF16), SSM_STATE)

    def out_map(cc):
        t = cc.transpose(0, 2, 1).reshape(SSM_CHUNKS, SSM_CHUNK_LANES, SSM_GROUP_IN)
        return block_diag(t.astype(BF16), SSM_GROUP_IN)

    def powers(base, n):
        tr, ti = base
        while tr.shape[0] < n:
            qr, qi = tr[-1:], ti[-1:]
            tr, ti = (jnp.concatenate([tr, tr * qr - ti * qi], axis=0),
                      jnp.concatenate([ti, tr * qi + ti * qr], axis=0))
        return tr, ti

    assert seg_len & (seg_len - 1) == 0
    a_row = (abr.reshape(1, SSM_LANES), abi.reshape(1, SSM_LANES))
    ones = jnp.ones((SUBLANES, 1), F32)
    a_b = jnp.stack([ones * a_row[0], ones * a_row[1]])
    pw_r, pw_i = powers(a_row, seg_len)
    sp_r, sp_i = powers((pw_r[-1:], pw_i[-1:]), SUBLANES)
    row = jnp.arange(SUBLANES)[:, None]
    seg = []
    for s in SCAN_STEPS:
        seg += [jnp.where(row >= s, sp_r[s - 1:s], 0.0), jnp.where(row >= s, sp_i[s - 1:s], 0.0)]
    seg += [sp_r, sp_i]
    pw = jnp.stack([pw_r, pw_i])
    return {"wb_re": in_map(bbr), "wb_im": in_map(bbi), "wc_re": out_map(c_re), "wc_im": out_map(c_im),
            "a": a_b, "seg": jnp.stack(seg), "pw": pw, "d": d_skip.reshape(1, SSM_WIDTH),
            "glu_w": glu_w.astype(BF16), "glu_b": glu_b.reshape(1, SSM_WIDTH)}


def _attend(q, k, v):
    s = lax.dot_general(q, k, (((1,), (1,)), ((), ())), preferred_element_type=F32)
    s = s * (XA_HEAD_DIM ** -0.5)
    p = jnp.exp(s - jnp.max(s, axis=-1, keepdims=True))
    p = p / jnp.sum(p, axis=-1, keepdims=True)
    return _dot(p.astype(BF16), v)


def _xa_prompt_kernel(q_ref, k_ref, v_ref, o_ref):
    for h in range(XA_HEADS):
        cols = slice(h * XA_HEAD_DIM, (h + 1) * XA_HEAD_DIM)
        o_ref[:, cols] = _attend(q_ref[:, cols], k_ref[:, cols], v_ref[:, cols]).astype(BF16)


def _xa_prompt(q, k, v, *, batch, seq, rows):
    nl = seq // rows
    return pl.pallas_call(
        _xa_prompt_kernel,
        grid=(batch, nl),
        in_specs=[pl.BlockSpec((rows, XA_WIDTH), lambda b, l: (b * nl + l, 0)),
                  pl.BlockSpec((N_MEM, XA_WIDTH), lambda b, l: (b, 0)),
                  pl.BlockSpec((N_MEM, XA_WIDTH), lambda b, l: (b, 0))],
        out_specs=pl.BlockSpec((rows, XA_WIDTH), lambda b, l: (b * nl + l, 0)),
        out_shape=jax.ShapeDtypeStruct((batch * seq, XA_WIDTH), BF16),
        compiler_params=_params("parallel", "arbitrary"),
        name="xattn_prompt",
    )(q, k, v)


KV_ROWS = N_MEM * XA_HALVES * XA_HEADS


def _kv_rows(cache):
    b = cache.shape[0]
    c = cache.reshape(b, N_MEM, XA_HEADS, XA_HALVES, LANES).transpose(0, 1, 3, 2, 4)
    return c.reshape(b, KV_ROWS, LANES)


def _kv_from_rows(rows):
    b = rows.shape[0]
    c = rows.reshape(b, N_MEM, XA_HALVES, XA_HEADS, LANES).transpose(0, 1, 3, 2, 4)
    return c.reshape(b, N_MEM, XA_HEADS, XA_HEAD_DIM)


def _memory_kv_kernel(x_ref, g_ref, wk_ref, wv_ref, k_ref, v_ref, kr_ref, vr_ref, xn_ref, *, batch):
    j = pl.program_id(0)

    @pl.when(j == 0)
    def _():
        _norm_to_scratch(x_ref, g_ref, xn_ref)

    for w_ref, o_ref, rows_ref in ((wk_ref, k_ref, kr_ref), (wv_ref, v_ref, vr_ref)):
        y = _dot(xn_ref[...], w_ref[...].astype(BF16))
        o_ref[...] = y.astype(BF16)
        for h in range(XA_HEADS):
            @pl.when(j == h)
            def _():
                for b in range(batch):
                    for t in range(XA_HALVES):
                        rows_ref[b, pl.ds(t * XA_HEADS + h, N_MEM, stride=XA_HALVES * XA_HEADS), :] = (
                            y[b * N_MEM:(b + 1) * N_MEM, t * LANES:(t + 1) * LANES])


def _memory_kv(mem, g, wk, wv):
    batch, n_mem, d = mem.shape
    rows_spec = pl.BlockSpec((batch, KV_ROWS, LANES), lambda j: (0, 0, 0))
    head_spec = pl.BlockSpec((batch * n_mem, XA_HEAD_DIM), lambda j: (0, j))
    w_spec = pl.BlockSpec((d, XA_HEAD_DIM), lambda j: (0, j))
    return pl.pallas_call(
        functools.partial(_memory_kv_kernel, batch=batch),
        grid=(XA_HEADS,),
        in_specs=[_resident((batch * n_mem, d)), _resident((1, d)), w_spec, w_spec],
        out_specs=[head_spec, head_spec, rows_spec, rows_spec],
        out_shape=[jax.ShapeDtypeStruct((batch * n_mem, XA_WIDTH), BF16)] * 2
        + [jax.ShapeDtypeStruct((batch, KV_ROWS, LANES), F32)] * 2,
        scratch_shapes=[pltpu.VMEM((batch * n_mem, d), BF16)],
        compiler_params=_params("arbitrary"),
        name="memory_kv",
    )(mem.reshape(batch * n_mem, d), g.reshape(1, d), wk, wv)


def _xa_sample_kernel(q_ref, k_ref, v_ref, o_ref, *, bb, seq):
    hl = XA_HEADS * seq
    n_rows = N_MEM * XA_HALVES * XA_HEADS
    col = lax.broadcasted_iota(jnp.int32, (hl, n_rows), 1)
    row = lax.broadcasted_iota(jnp.int32, (hl, n_rows), 0)
    slot = col & (XA_HALVES * XA_HEADS - 1)
    head = row // seq
    first_half = slot == head
    second_half = slot == head + XA_HEADS
    q_all = q_ref[...].astype(F32)
    for b in range(bb):
        rows = slice(b * seq, (b + 1) * seq)
        q = q_all[rows, :]
        qm = jnp.concatenate([q[:, h * XA_HEAD_DIM + t * LANES:h * XA_HEAD_DIM + (t + 1) * LANES]
                              for t in range(XA_HALVES) for h in range(XA_HEADS)], axis=0)
        g = lax.dot_general(qm.astype(BF16), k_ref[b].astype(BF16), (((1,), (1,)), ((), ())),
                            preferred_element_type=F32)
        g0 = jnp.where(first_half, g[:hl], 0.0)
        g1 = jnp.where(second_half, g[hl:], 0.0)
        s = (g0 + pltpu.roll(g1, n_rows - XA_HEADS, 1)) * (XA_HEAD_DIM ** -0.5)
        s = jnp.where(first_half, s, -jnp.inf)
        p = jnp.exp(s - jnp.max(s, axis=-1, keepdims=True))
        p = p / jnp.sum(p, axis=-1, keepdims=True)
        pp = jnp.concatenate([p, pltpu.roll(p, XA_HEADS, 1)], axis=0).astype(BF16)
        o = _dot(pp, v_ref[b].astype(BF16))
        for t in range(XA_HALVES):
            for h in range(XA_HEADS):
                r0 = t * hl + h * seq
                c0 = h * XA_HEAD_DIM + t * LANES
                o_ref[rows, c0:c0 + LANES] = o[r0:r0 + seq].astype(BF16)


def _xa_sample(z, mem_k, mem_v, *, batch, seq, bb):
    n_rows = N_MEM * XA_HALVES * XA_HEADS
    return pl.pallas_call(
        functools.partial(_xa_sample_kernel, bb=bb, seq=seq),
        grid=(batch // bb,),
        in_specs=[pl.BlockSpec((bb * seq, XA_WIDTH), lambda i: (i, 0)),
                  pl.BlockSpec((bb, n_rows, LANES), lambda i: (i, 0, 0)),
                  pl.BlockSpec((bb, n_rows, LANES), lambda i: (i, 0, 0))],
        out_specs=pl.BlockSpec((bb * seq, XA_WIDTH), lambda i: (i, 0)),
        out_shape=jax.ShapeDtypeStruct((batch * seq, XA_WIDTH), BF16),
        compiler_params=_params("parallel"),
        name="xattn_sample",
    )(z, mem_k, mem_v)


def _merge_kernel(ap_ref, as_ref, ax_ref, g0_ref, g1_ref, g2_ref, x_ref, wp_ref, ws_ref, wx_ref, wo_ref, o_ref):
    merged = (g0_ref[...] * _dot(ap_ref[...], wp_ref[...])
              + g1_ref[...] * _dot(as_ref[...], ws_ref[...])
              + g2_ref[...] * _dot(ax_ref[...], wx_ref[...]))
    o_ref[...] = x_ref[...] + _dot(merged.astype(BF16), wo_ref[...])


def _merge(a_pool, a_ssm, a_xa, gates, x, wp, ws, wx, wo, *, tm):
    m, d = x.shape
    act = pl.BlockSpec((tm, a_pool.shape[1]), lambda i: (i, 0))
    return pl.pallas_call(
        _merge_kernel,
        grid=(m // tm,),
        in_specs=[act, act, act,
                  pl.BlockSpec((tm, d), lambda i: (i, 0)),
                  pl.BlockSpec((tm, d), lambda i: (i, 1)),
                  pl.BlockSpec((tm, d), lambda i: (i, 2)),
                  pl.BlockSpec((tm, d), lambda i: (i, 0)),
                  _resident(wp.shape), _resident(ws.shape), _resident(wx.shape), _resident(wo.shape)],
        out_specs=pl.BlockSpec((tm, d), lambda i: (i, 0)),
        out_shape=jax.ShapeDtypeStruct((m, d), F32),
        compiler_params=_params("parallel"),
        name="merge",
    )(a_pool, a_ssm, a_xa, gates, gates, gates, x, wp, ws, wx, wo)


def _mlp_kernel(x_ref, g_ref, w1_ref, w2_ref, gf_ref, o_ref, *rest, emit_w):
    if emit_w:
        w1b_ref, w2b_ref, xn_ref = rest
        w1b_ref[...] = w1_ref[...].astype(BF16)
        w2b_ref[...] = w2_ref[...].astype(BF16)
    else:
        (xn_ref,) = rest
        w1b_ref, w2b_ref = w1_ref, w2_ref
    f = pl.program_id(1)

    @pl.when(f == 0)
    def _():
        _norm_to_scratch(x_ref, g_ref, xn_ref)
        o_ref[...] = x_ref[...]

    h = jnp.square(jnp.maximum(_dot(xn_ref[...], w1b_ref[...]), 0.0))
    o_ref[...] += _dot(h.astype(BF16), w2b_ref[...])

    @pl.when(f == pl.num_programs(1) - 1)
    def _():
        y = o_ref[...]
        o_ref[...] = (y * _rms_scale(y)) * gf_ref[...]


def _mlp(x, g, w1, w2, gf, *, tm, tf, name):
    m, d = x.shape
    dff = w1.shape[1]
    emit_w = w1.dtype != BF16
    assert not emit_w or m == tm, "the bf16 weight copies are written once, by a single row tile"
    w1_spec = pl.BlockSpec((d, tf), lambda i, f: (0, f))
    w2_spec = pl.BlockSpec((tf, d), lambda i, f: (f, 0))
    out_specs = [pl.BlockSpec((tm, d), lambda i, f: (i, 0))]
    out_shape = [jax.ShapeDtypeStruct((m, d), F32)]
    if emit_w:
        out_specs += [w1_spec, w2_spec]
        out_shape += [jax.ShapeDtypeStruct(w1.shape, BF16), jax.ShapeDtypeStruct(w2.shape, BF16)]
    return pl.pallas_call(
        functools.partial(_mlp_kernel, emit_w=emit_w),
        grid=(m // tm, dff // tf),
        in_specs=[pl.BlockSpec((tm, d), lambda i, f: (i, 0)),
                  pl.BlockSpec((1, d), lambda i, f: (0, 0)),
                  w1_spec, w2_spec,
                  pl.BlockSpec((1, d), lambda i, f: (0, 0))],
        out_specs=out_specs,
        out_shape=out_shape,
        scratch_shapes=[pltpu.VMEM((tm, d), BF16)],
        compiler_params=_params("parallel", "arbitrary"),
        name=name,
    )(x, g.reshape(1, d), w1, w2, gf.reshape(1, d))


def _layer_weights(l, norm1_g, w_in, b_gate, pool_w, pool_scale, pool_proj, ssm_A_re, ssm_A_im, ssm_log_dt,
                   ssm_B_re, ssm_B_im, ssm_C_re, ssm_C_im, ssm_D, ssm_glu_w, ssm_glu_b, ssm_proj,
                   mem_norm_g, xa_wk, xa_wv, xa_wo, w_out, norm2_g, mlp_w1, mlp_w2):
    return {
        "norm1_g": norm1_g[l], "w_in": w_in[l], "b_gate": b_gate[l].reshape(-1),
        "pool_w": pool_w[l].astype(BF16), "pool_scale": pool_scale[l].reshape(1, POOL_WIDTH),
        "pool_proj": pool_proj[l].astype(BF16),
        "ssm": _ssm_weights(ssm_A_re[l], ssm_A_im[l], ssm_log_dt[l], ssm_B_re[l], ssm_B_im[l],
                            ssm_C_re[l], ssm_C_im[l], ssm_D[l], ssm_glu_w[l], ssm_glu_b[l],
                            seg_len=SSM_PROMPT_ROWS // SUBLANES),
        "ssm_proj": ssm_proj[l].astype(BF16),
        "mem_norm_g": mem_norm_g[l],
        "xa_wk": xa_wk[l], "xa_wv": xa_wv[l],
        "xa_wo": xa_wo[l].astype(BF16), "w_out": w_out[l].astype(BF16),
        "norm2_g": norm2_g[l], "mlp_w1": mlp_w1[l], "mlp_w2": mlp_w2[l],
    }


def _merge_branches(x, gates, a_pool, a_ssm, a_xa, w):
    return _merge(a_pool, a_ssm, a_xa, gates, x, w["pool_proj"], w["ssm_proj"], w["xa_wo"], w["w_out"], tm=MERGE_TM)


def kernel(x_prompt, x_sample, state_pool, state_ssm_re, state_ssm_im, cache_mem_k, cache_mem_v, mem_prompt,
           norm1_g, w_in, b_gate, pool_w, pool_scale, pool_proj, ssm_A_re, ssm_A_im, ssm_log_dt, ssm_B_re,
           ssm_B_im, ssm_C_re, ssm_C_im, ssm_D, ssm_glu_w, ssm_glu_b, ssm_proj, mem_norm_g, xa_wk, xa_wv,
           xa_wo, w_out, norm2_g, mlp_w1, mlp_w2, final_norm_g):
    depth = norm1_g.shape[0]
    assert depth == 1, "the final norm is fused into the single layer's MLP kernel"
    bp, lp, d = x_prompt.shape
    bs, ls, _ = x_sample.shape
    n_prev_sample = min(POOL_BUF, PAST_LEN)
    w = _layer_weights(0, norm1_g, w_in, b_gate, pool_w, pool_scale, pool_proj, ssm_A_re, ssm_A_im, ssm_log_dt,
                       ssm_B_re, ssm_B_im, ssm_C_re, ssm_C_im, ssm_D, ssm_glu_w, ssm_glu_b, ssm_proj,
                       mem_norm_g, xa_wk, xa_wv, xa_wo, w_out, norm2_g, mlp_w1, mlp_w2)

    xs = x_sample.reshape(bs * ls, d)
    assert xs.shape[0] == IN_TM == MLP_TM
    us, qs, gs, w_in_b = _in_proj(xs, w["norm1_g"], w["w_in"], w["b_gate"], tm=IN_TM, tn=IN_SAMPLE_TN,
                                  name="in_proj_sample")
    a_pool, pool_s = _pool_sample(us, state_pool, w["pool_w"], w["pool_scale"], batch=bs, seq=ls,
                                  bb=POOL_SAMPLE_SEQS, n_prev=n_prev_sample)
    a_ssm, re_s, im_s = _ssm_sample(us, state_ssm_re[0].reshape(bs, SSM_LANES),
                                    state_ssm_im[0].reshape(bs, SSM_LANES), w["ssm"], batch=bs, seq=ls,
                                    bb=SSM_SAMPLE_SEQS)
    a_xa = _xa_sample(qs, _kv_rows(cache_mem_k[0]), _kv_rows(cache_mem_v[0]), batch=bs, seq=ls, bb=XA_SAMPLE_SEQS)
    xs2 = _merge_branches(xs, gs, a_pool, a_ssm, a_xa, w)
    y_sample, w1_b, w2_b = _mlp(xs2, w["norm2_g"], w["mlp_w1"], w["mlp_w2"], final_norm_g, tm=MLP_TM,
                                tf=MLP_SAMPLE_TF, name="mlp_sample")
    y_sample = y_sample.reshape(bs, ls, d)

    xp = x_prompt.reshape(bp * lp, d)
    kp, vp, mk_rows, mv_rows = _memory_kv(mem_prompt, w["mem_norm_g"], w["xa_wk"], w["xa_wv"])
    up, qp, gp = _in_proj(xp, w["norm1_g"], w_in_b, w["b_gate"], tm=IN_TM, tn=IN_TN, name="in_proj_prompt")
    a_pool, pool_p = _pool_prompt(up, w["pool_w"], w["pool_scale"], batch=bp, seq=lp, rows=POOL_PROMPT_ROWS)
    a_ssm, re_p, im_p = _ssm_prompt(up, w["ssm"], batch=bp, seq=lp, rows=SSM_PROMPT_ROWS)
    a_xa = _xa_prompt(qp, kp, vp, batch=bp, seq=lp, rows=XA_PROMPT_ROWS)
    xp2 = _merge_branches(xp, gp, a_pool, a_ssm, a_xa, w)
    (y_prompt,) = _mlp(xp2, w["norm2_g"], w1_b, w2_b, final_norm_g, tm=MLP_TM, tf=MLP_TF, name="mlp_prompt")
    y_prompt = y_prompt.reshape(bp, lp, d)

    state_shape = (1, -1, SSM_GROUPS, SSM_STATE)
    return (y_prompt, y_sample,
            pool_p, re_p.reshape(state_shape), im_p.reshape(state_shape),
            _kv_from_rows(mk_rows)[None], _kv_from_rows(mv_rows)[None],
            pool_s, re_s.reshape(state_shape), im_s.reshape(state_shape))
```

```python
import functools
import math

import jax
import jax.numpy as jnp
from jax import lax
from jax.experimental import pallas as pl
from jax.experimental.pallas import tpu as pltpu

F32 = jnp.float32
BF16 = jnp.bfloat16

D_MODEL = 2048
POOL_WINDOWS = (2, 4, 8, 16)
POOL_GROUP_W = 256
POOL_WIDTH = 1024
POOL_BUF = 15
SSM_WIDTH = 1024
SSM_GROUPS = 64
SSM_GROUP_IN = 16
SSM_STATE = 64
SSM_LANES = SSM_GROUPS * SSM_STATE
SSM_CHUNKS = 4
SSM_CHUNK_GROUPS = SSM_GROUPS // SSM_CHUNKS
SSM_CHUNK_IN = SSM_WIDTH // SSM_CHUNKS
SSM_CHUNK_LANES = SSM_LANES // SSM_CHUNKS
XA_HEADS = 4
XA_HEAD_DIM = 256
XA_WIDTH = 1024
N_MEM = 256
N_BRANCH = 3
D_FF = 4 * D_MODEL
EPS = 1e-6
PAST_LEN = 16384

SUBLANES = 8
LANES = 128
XA_HALVES = XA_HEAD_DIM // LANES
POOL_HIST = 2 * SUBLANES
SCAN_STEPS = (1, 2, 4)
SCAN_LANE_BLOCK = 512
SSM_PROMPT_ROWS = 256

IN_TM = 1024
IN_TN = 1024
IN_SAMPLE_TN = 512
MERGE_TM = 256
MLP_TM = 1024
MLP_TF = 512
POOL_PROMPT_ROWS = 512
XA_PROMPT_ROWS = 512
POOL_SAMPLE_SEQS = 16
SSM_SAMPLE_SEQS = 32
XA_SAMPLE_SEQS = 4
VMEM_LIMIT = 56 * 1024 * 1024


def _params(*sem):
    return pltpu.CompilerParams(dimension_semantics=sem, vmem_limit_bytes=VMEM_LIMIT)


def _resident(shape):
    nd = len(shape)
    return pl.BlockSpec(shape, lambda *_: (0,) * nd, pipeline_mode=pl.Buffered(1))


def _rms_scale(x):
    return lax.rsqrt(jnp.mean(x * x, axis=-1, keepdims=True) + EPS)


def _dot(a, b):
    return jnp.dot(a, b, preferred_element_type=F32)


def _norm_to_scratch(x_ref, g_ref, xn_ref):
    x = x_ref[...]
    xn_ref[...] = ((x * _rms_scale(x)) * g_ref[...]).astype(BF16)


def _in_proj_kernel(x_ref, g_ref, w_ref, b_ref, u_ref, q_ref, gate_ref, *rest, u_blocks, gate0, emit_w):
    if emit_w:
        wb_ref, xn_ref = rest
    else:
        (xn_ref,) = rest
    j = pl.program_id(1)

    @pl.when(j == 0)
    def _():
        _norm_to_scratch(x_ref, g_ref, xn_ref)

    if emit_w:
        wb_ref[...] = w_ref[...].astype(BF16)

    def w_block():
        return wb_ref[...] if emit_w else w_ref[...]

    @pl.when(j < u_blocks)
    def _():
        u_ref[...] = _dot(xn_ref[...], w_block())

    @pl.when(jnp.logical_and(j >= u_blocks, j < gate0))
    def _():
        q_ref[...] = _dot(xn_ref[...], w_block()).astype(BF16)

    @pl.when(j >= gate0)
    def _():
        gate_ref[...] = jax.nn.sigmoid(_dot(xn_ref[...], w_block()) + b_ref[...]).astype(BF16)


def _in_proj(x, g, w_in, b_gate, *, tm, tn, name):
    m, d = x.shape
    n = w_in.shape[1]
    emit_w = w_in.dtype != BF16
    assert not emit_w or m == tm, "the bf16 weight copy is written once, by a single row tile"
    u_blocks = (POOL_WIDTH + SSM_WIDTH) // tn
    q_blocks = XA_WIDTH // tn
    gate0 = u_blocks + q_blocks
    n_gate = n // tn - gate0
    out_specs = [pl.BlockSpec((tm, tn), lambda i, j: (i, jnp.minimum(j, u_blocks - 1))),
                 pl.BlockSpec((tm, tn), lambda i, j: (i, jnp.clip(j - u_blocks, 0, q_blocks - 1))),
                 pl.BlockSpec((tm, tn), lambda i, j: (i, jnp.maximum(j - gate0, 0)))]
    out_shape = [jax.ShapeDtypeStruct((m, u_blocks * tn), F32),
                 jax.ShapeDtypeStruct((m, q_blocks * tn), BF16),
                 jax.ShapeDtypeStruct((m, n_gate * tn), BF16)]
    if emit_w:
        out_specs.append(pl.BlockSpec((d, tn), lambda i, j: (0, j)))
        out_shape.append(jax.ShapeDtypeStruct((d, n), BF16))
    return pl.pallas_call(
        functools.partial(_in_proj_kernel, u_blocks=u_blocks, gate0=gate0, emit_w=emit_w),
        grid=(m // tm, n // tn),
        in_specs=[pl.BlockSpec((tm, d), lambda i, j: (i, 0)),
                  pl.BlockSpec((1, d), lambda i, j: (0, 0)),
                  pl.BlockSpec((d, tn), lambda i, j: (0, j)),
                  pl.BlockSpec((1, tn), lambda i, j: (0, jnp.maximum(j - gate0, 0)))],
        out_specs=out_specs,
        out_shape=out_shape,
        scratch_shapes=[pltpu.VMEM((tm, d), BF16)],
        compiler_params=_params("parallel", "arbitrary"),
        name=name,
    )(x, g.reshape(1, d), w_in, b_gate.reshape(1, n_gate * tn))


def _window_means(e_ref, row0, rows, first_pos):
    outs = []
    for k, wdw in enumerate(POOL_WINDOWS):
        lo = k * POOL_GROUP_W
        cur = e_ref[row0:row0 + rows, lo:lo + POOL_GROUP_W]
        s = cur
        for d in range(1, wdw):
            s = s + e_ref[row0 - d:row0 - d + rows, lo:lo + POOL_GROUP_W]
        pos = lax.broadcasted_iota(jnp.int32, (rows, POOL_GROUP_W), 0) + first_pos
        cnt = jnp.minimum(pos, wdw).astype(F32)
        outs.append(s / cnt - cur)
    return outs


def _pool_mix(pooled, k, pw_ref, ps_ref):
    lo = k * POOL_GROUP_W
    mixed = _dot(pooled.astype(BF16), pw_ref[k])
    return (mixed * ps_ref[:, lo:lo + POOL_GROUP_W]).astype(BF16)


def _pool_prompt_kernel(u_ref, pw_ref, ps_ref, o_ref, st_ref, e_ref, *, rows):
    l = pl.program_id(1)

    @pl.when(l == 0)
    def _():
        e_ref[0:POOL_HIST, :] = jnp.zeros((POOL_HIST, POOL_WIDTH), F32)

    @pl.when(l > 0)
    def _():
        e_ref[0:POOL_HIST, :] = e_ref[rows:rows + POOL_HIST, :]

    e_ref[POOL_HIST:POOL_HIST + rows, :] = u_ref[...]
    pooled = _window_means(e_ref, POOL_HIST, rows, l * rows + 1)
    for k, p in enumerate(pooled):
        o_ref[:, k * POOL_GROUP_W:(k + 1) * POOL_GROUP_W] = _pool_mix(p, k, pw_ref, ps_ref)

    @pl.when(l == pl.num_programs(1) - 1)
    def _():
        st_ref[0] = e_ref[POOL_HIST + rows - POOL_BUF:POOL_HIST + rows, :]


def _pool_prompt(z, pool_w, pool_scale, *, batch, seq, rows):
    nl = seq // rows
    return pl.pallas_call(
        functools.partial(_pool_prompt_kernel, rows=rows),
        grid=(batch, nl),
        in_specs=[pl.BlockSpec((rows, POOL_WIDTH), lambda b, l: (b * nl + l, 0)),
                  _resident(pool_w.shape), _resident(pool_scale.shape)],
        out_specs=[pl.BlockSpec((rows, POOL_WIDTH), lambda b, l: (b * nl + l, 0)),
                   pl.BlockSpec((None, 1, POOL_BUF, POOL_WIDTH), lambda b, l: (0, b, 0, 0))],
        out_shape=[jax.ShapeDtypeStruct((batch * seq, POOL_WIDTH), BF16),
                   jax.ShapeDtypeStruct((1, batch, POOL_BUF, POOL_WIDTH), F32)],
        scratch_shapes=[pltpu.VMEM((rows + POOL_HIST, POOL_WIDTH), F32)],
        compiler_params=_params("parallel", "arbitrary"),
        name="pool_prompt",
    )(z, pool_w, pool_scale)


def _pool_sample_kernel(u_ref, prev_ref, pw_ref, ps_ref, o_ref, st_ref, e_ref, p_ref, *, bb, seq, n_prev):
    span = POOL_HIST + seq
    for b in range(bb):
        top = b * span + POOL_HIST
        e_ref[top - POOL_BUF:top, :] = prev_ref[b]
        e_ref[top:top + seq, :] = u_ref[b * seq:(b + 1) * seq, :]
    for b in range(bb):
        top = b * span + POOL_HIST
        st_ref[b] = e_ref[top + seq - POOL_BUF:top + seq, :]
        pooled = _window_means(e_ref, top, seq, n_prev + 1)
        for k, p in enumerate(pooled):
            p_ref[b * seq:(b + 1) * seq, k * POOL_GROUP_W:(k + 1) * POOL_GROUP_W] = p
    for k in range(len(POOL_WINDOWS)):
        lo = k * POOL_GROUP_W
        o_ref[:, lo:lo + POOL_GROUP_W] = _pool_mix(p_ref[:, lo:lo + POOL_GROUP_W], k, pw_ref, ps_ref)


def _pool_sample(z, prev, pool_w, pool_scale, *, batch, seq, bb, n_prev):
    return pl.pallas_call(
        functools.partial(_pool_sample_kernel, bb=bb, seq=seq, n_prev=n_prev),
        grid=(batch // bb,),
        in_specs=[pl.BlockSpec((bb * seq, POOL_WIDTH), lambda i: (i, 0)),
                  pl.BlockSpec((None, bb, POOL_BUF, POOL_WIDTH), lambda i: (0, i, 0, 0)),
                  _resident(pool_w.shape), _resident(pool_scale.shape)],
        out_specs=[pl.BlockSpec((bb * seq, POOL_WIDTH), lambda i: (i, 0)),
                   pl.BlockSpec((None, bb, POOL_BUF, POOL_WIDTH), lambda i: (0, i, 0, 0))],
        out_shape=[jax.ShapeDtypeStruct((batch * seq, POOL_WIDTH), BF16),
                   jax.ShapeDtypeStruct((1, batch, POOL_BUF, POOL_WIDTH), F32)],
        scratch_shapes=[pltpu.VMEM((bb * (POOL_HIST + seq), POOL_WIDTH), F32),
                        pltpu.VMEM((bb * seq, POOL_WIDTH), F32)],
        compiler_params=_params("parallel"),
        name="pool_sample",
    )(z, prev, pool_w, pool_scale)


def _regroup_matrix(rows, seg_len):
    dst = jnp.arange(rows)
    g, j, s = dst // (SUBLANES * seg_len), (dst // SUBLANES) % seg_len, dst % SUBLANES
    src = (g * SUBLANES + s) * seg_len + j
    return (src[:, None] == jnp.arange(rows)[None, :]).astype(BF16)


def _regroup_rows(u_ref, p_ref, up_ref):
    up_ref[...] = _dot(p_ref[...], u_ref[...].astype(BF16)).astype(BF16)


def _scatter_rows(y, tile_ref, tile0, groups, seg_len):
    for g in range(groups):
        for j in range(seg_len):
            r = (g * seg_len + j) * SUBLANES
            for c in range(y.shape[1] // LANES):
                tile_ref[tile0 + c, pl.ds(g * SUBLANES * seg_len + j, SUBLANES, stride=seg_len), :] = (
                    y[r:r + SUBLANES, c * LANES:(c + 1) * LANES])


def _ssm_input_map(up_ref, wbr_ref, wbi_ref, hr_ref, hi_ref):
    for c in range(SSM_CHUNKS):
        uc = up_ref[:, c * SSM_CHUNK_IN:(c + 1) * SSM_CHUNK_IN]
        lanes = slice(c * SSM_CHUNK_LANES, (c + 1) * SSM_CHUNK_LANES)
        hr_ref[:, lanes] = _dot(uc, wbr_ref[c])
        hi_ref[:, lanes] = _dot(uc, wbi_ref[c])


def _cmul_add(ar, ai, hr, hi, xr, xi):
    return xr + (ar * hr - ai * hi), xi + (ar * hi + ai * hr)


def _scan_rows(xr, xi, coef, car_r, car_i):
    for n, s in enumerate(SCAN_STEPS):
        xr, xi = _cmul_add(coef[2 * n], coef[2 * n + 1], pltpu.roll(xr, s, 0), pltpu.roll(xi, s, 0), xr, xi)
    return _cmul_add(coef[2 * len(SCAN_STEPS)], coef[2 * len(SCAN_STEPS) + 1], car_r, car_i, xr, xi)


def _last_row(x):
    return jnp.broadcast_to(x[SUBLANES - 1:SUBLANES, :], x.shape)


def _ssm_output(u_ref, hbr_ref, hbi_ref, wcr_ref, wci_ref, d_ref, gw_ref, gb_ref, tile_ref, zg_ref, o_ref,
                groups, seg_len):
    for c in range(SSM_CHUNKS):
        lanes = slice(c * SSM_CHUNK_LANES, (c + 1) * SSM_CHUNK_LANES)
        y = _dot(hbr_ref[:, lanes], wcr_ref[c]) - _dot(hbi_ref[:, lanes], wci_ref[c])
        _scatter_rows(y, tile_ref, c * (SSM_CHUNK_IN // LANES), groups, seg_len)
    for c in range(SSM_WIDTH // LANES):
        cols = slice(c * LANES, (c + 1) * LANES)
        zg_ref[:, cols] = jax.nn.gelu(tile_ref[c] + d_ref[:, cols] * u_ref[:, cols])
    zg = zg_ref[...]
    gate = jax.nn.sigmoid(_dot(zg.astype(BF16), gw_ref[...]) + gb_ref[...])
    o_ref[...] = (zg * gate).astype(BF16)


def _ssm_prompt_kernel(u_ref, p_ref, wbr_ref, wbi_ref, wcr_ref, wci_ref, a_ref, seg_ref, pw_ref, d_ref, gw_ref,
                       gb_ref, o_ref, sr_ref, si_ref, tile_ref, up_ref, hr_ref, hi_ref, hbr_ref, hbi_ref, zg_ref,
                       car_ref, *, rows):
    seg_len = rows // SUBLANES
    width = SCAN_LANE_BLOCK

    @pl.when(pl.program_id(1) == 0)
    def _():
        car_ref[...] = jnp.zeros(car_ref.shape, F32)

    _regroup_rows(u_ref, p_ref, up_ref)
    _ssm_input_map(up_ref, wbr_ref, wbi_ref, hr_ref, hi_ref)

    for lb in range(SSM_LANES // width):
        lanes = slice(lb * width, (lb + 1) * width)
        a_r, a_i = a_ref[0, :, lanes], a_ref[1, :, lanes]

        h_r = h_i = jnp.zeros((SUBLANES, width), F32)
        for j in range(seg_len):
            rs = slice(j * SUBLANES, (j + 1) * SUBLANES)
            h_r, h_i = _cmul_add(a_r, a_i, h_r, h_i, hr_ref[rs, lanes], hi_ref[rs, lanes])
            hr_ref[rs, lanes] = h_r
            hi_ref[rs, lanes] = h_i

        car_r, car_i = car_ref[0, :, lanes], car_ref[1, :, lanes]
        seg = tuple(seg_ref[n, :, lanes] for n in range(2 * len(SCAN_STEPS) + 2))
        end_r, end_i = _scan_rows(h_r, h_i, seg, car_r, car_i)
        car_ref[0, :, lanes] = _last_row(end_r)
        car_ref[1, :, lanes] = _last_row(end_i)
        sr_ref[0, :, lanes] = end_r[SUBLANES - 1:SUBLANES, :]
        si_ref[0, :, lanes] = end_i[SUBLANES - 1:SUBLANES, :]
        first = lax.broadcasted_iota(jnp.int32, (SUBLANES, width), 0) == 0
        in_r = jnp.where(first, car_r, pltpu.roll(end_r, 1, 0))
        in_i = jnp.where(first, car_i, pltpu.roll(end_i, 1, 0))

        for jj in range(seg_len // 2):
            out_r, out_i = [], []
            for j in (2 * jj, 2 * jj + 1):
                rs = slice(j * SUBLANES, (j + 1) * SUBLANES)
                xr, xi = _cmul_add(pw_ref[0, rs, lanes], pw_ref[1, rs, lanes], in_r, in_i,
                                   hr_ref[rs, lanes], hi_ref[rs, lanes])
                out_r.append(xr)
                out_i.append(xi)
            rp = slice(jj * 2 * SUBLANES, (jj + 1) * 2 * SUBLANES)
            hbr_ref[rp, lanes] = jnp.concatenate(out_r, axis=0).astype(BF16)
            hbi_ref[rp, lanes] = jnp.concatenate(out_i, axis=0).astype(BF16)

    _ssm_output(u_ref, hbr_ref, hbi_ref, wcr_ref, wci_ref, d_ref, gw_ref, gb_ref, tile_ref, zg_ref, o_ref,
                1, seg_len)


def _ssm_common_specs(w, rows):
    return [_resident((rows, rows)), _resident(w["wb_re"].shape), _resident(w["wb_im"].shape),
            _resident(w["wc_re"].shape), _resident(w["wc_im"].shape), _resident(w["a"].shape)]


def _ssm_tail_specs(w):
    return [_resident(w["d"].shape), _resident(w["glu_w"].shape), _resident(w["glu_b"].shape)]


def _ssm_scratch(rows):
    return [pltpu.VMEM((SSM_WIDTH // LANES, rows, LANES), F32), pltpu.VMEM((rows, SSM_WIDTH), BF16),
            pltpu.VMEM((rows, SSM_LANES), F32), pltpu.VMEM((rows, SSM_LANES), F32),
            pltpu.VMEM((rows, SSM_LANES), BF16), pltpu.VMEM((rows, SSM_LANES), BF16),
            pltpu.VMEM((rows, SSM_WIDTH), F32)]


def _ssm_prompt(z, w, *, batch, seq, rows):
    nl = seq // rows
    return pl.pallas_call(
        functools.partial(_ssm_prompt_kernel, rows=rows),
        grid=(batch, nl),
        in_specs=([pl.BlockSpec((rows, SSM_WIDTH), lambda b, l: (b * nl + l, 1))] + _ssm_common_specs(w, rows)
                  + [_resident(w["seg"].shape), _resident(w["pw"].shape)] + _ssm_tail_specs(w)),
        out_specs=[pl.BlockSpec((rows, SSM_WIDTH), lambda b, l: (b * nl + l, 0)),
                   pl.BlockSpec((1, 1, SSM_LANES), lambda b, l: (b, 0, 0)),
                   pl.BlockSpec((1, 1, SSM_LANES), lambda b, l: (b, 0, 0))],
        out_shape=[jax.ShapeDtypeStruct((batch * seq, SSM_WIDTH), BF16),
                   jax.ShapeDtypeStruct((batch, 1, SSM_LANES), F32),
                   jax.ShapeDtypeStruct((batch, 1, SSM_LANES), F32)],
        scratch_shapes=_ssm_scratch(rows) + [pltpu.VMEM((2, SUBLANES, SSM_LANES), F32)],
        compiler_params=_params("parallel", "arbitrary"),
        name="ssm_prompt",
    )(z, _regroup_matrix(rows, rows // SUBLANES), w["wb_re"], w["wb_im"], w["wc_re"], w["wc_im"], w["a"],
      w["seg"], w["pw"], w["d"], w["glu_w"], w["glu_b"])


def _ssm_sample_kernel(u_ref, h0r_ref, h0i_ref, p_ref, wbr_ref, wbi_ref, wcr_ref, wci_ref, a_ref, d_ref, gw_ref,
                       gb_ref, o_ref, sr_ref, si_ref, tile_ref, up_ref, hr_ref, hi_ref, hbr_ref, hbi_ref, zg_ref,
                       *, bb, seq):
    groups = bb // SUBLANES
    width = SCAN_LANE_BLOCK
    _regroup_rows(u_ref, p_ref, up_ref)
    _ssm_input_map(up_ref, wbr_ref, wbi_ref, hr_ref, hi_ref)

    for lb in range(SSM_LANES // width):
        lanes = slice(lb * width, (lb + 1) * width)
        a_r, a_i = a_ref[0, :, lanes], a_ref[1, :, lanes]
        for g in range(groups):
            seqs = slice(g * SUBLANES, (g + 1) * SUBLANES)
            h_r, h_i = h0r_ref[seqs, lanes], h0i_ref[seqs, lanes]
            for jj in range(seq // 2):
                out_r, out_i = [], []
                for d in range(2):
                    r0 = (g * seq + 2 * jj + d) * SUBLANES
                    h_r, h_i = _cmul_add(a_r, a_i, h_r, h_i, hr_ref[r0:r0 + SUBLANES, lanes],
                                         hi_ref[r0:r0 + SUBLANES, lanes])
                    out_r.append(h_r)
                    out_i.append(h_i)
                r16 = (g * seq + 2 * jj) * SUBLANES
                hbr_ref[r16:r16 + 2 * SUBLANES, lanes] = jnp.concatenate(out_r, axis=0).astype(BF16)
                hbi_ref[r16:r16 + 2 * SUBLANES, lanes] = jnp.concatenate(out_i, axis=0).astype(BF16)
            sr_ref[seqs, lanes] = h_r
            si_ref[seqs, lanes] = h_i

    _ssm_output(u_ref, hbr_ref, hbi_ref, wcr_ref, wci_ref, d_ref, gw_ref, gb_ref, tile_ref, zg_ref, o_ref,
                groups, seq)


def _ssm_sample(z, h_re, h_im, w, *, batch, seq, bb):
    assert seq % 2 == 0 and bb % SUBLANES == 0
    rows = bb * seq
    return pl.pallas_call(
        functools.partial(_ssm_sample_kernel, bb=bb, seq=seq),
        grid=(batch // bb,),
        in_specs=([pl.BlockSpec((rows, SSM_WIDTH), lambda i: (i, 1)),
                   pl.BlockSpec((bb, SSM_LANES), lambda i: (i, 0)),
                   pl.BlockSpec((bb, SSM_LANES), lambda i: (i, 0))] + _ssm_common_specs(w, rows)
                  + _ssm_tail_specs(w)),
        out_specs=[pl.BlockSpec((rows, SSM_WIDTH), lambda i: (i, 0)),
                   pl.BlockSpec((bb, SSM_LANES), lambda i: (i, 0)),
                   pl.BlockSpec((bb, SSM_LANES), lambda i: (i, 0))],
        out_shape=[jax.ShapeDtypeStruct((batch * seq, SSM_WIDTH), BF16),
                   jax.ShapeDtypeStruct((batch, SSM_LANES), F32),
                   jax.ShapeDtypeStruct((batch, SSM_LANES), F32)],
        scratch_shapes=_ssm_scratch(rows),
        compiler_params=_params("parallel"),
        name="ssm_sample",
    )(z, h_re, h_im, _regroup_matrix(rows, seq), w["wb_re"], w["wb_im"], w["wc_re"], w["wc_im"], w["a"], w["d"],
      w["glu_w"], w["glu_b"])


def _ssm_weights(a_re, a_im, log_dt, b_re, b_im, c_re, c_im, d_skip, glu_w, glu_b, *, seg_len):
    dt = jnp.exp(log_dt)[:, None]
    mag = jnp.exp(dt * a_re)
    abr, abi = mag * jnp.cos(dt * a_im), mag * jnp.sin(dt * a_im)
    den = a_re * a_re + a_im * a_im
    xr, xi = abr - 1.0, abi
    fr = (xr * a_re + xi * a_im) / den
    fi = (xi * a_re - xr * a_im) / den
    bbr = fr[..., None] * b_re - fi[..., None] * b_im
    bbi = fr[..., None] * b_im + fi[..., None] * b_re

    def block_diag(t, width):
        rows = t.shape[1]
        per_group = rows // SSM_CHUNK_GROUPS
        cols = SSM_CHUNK_GROUPS * width
        tile = (jnp.arange(width)[:, None] == jnp.arange(cols)[None, :] % width).astype(BF16)
        tiled = jnp.einsum("crk,kn->crn", t, tile, preferred_element_type=F32)
        same = (jnp.arange(rows)[:, None] // per_group) == (jnp.arange(cols)[None, :] // width)
        return jnp.where(same[None], tiled, 0.0).astype(BF16)

    def in_map(bb):
        t = bb.transpose(0, 2, 1).reshape(SSM_CHUNKS, SSM_CHUNK_IN, SSM_STATE)
        return block_diag(t.astype(BF16), SSM_STATE)

    def out_map(cc):
        t = cc.transpose(0, 2, 1).reshape(SSM_CHUNKS, SSM_CHUNK_LANES, SSM_GROUP_IN)
        return block_diag(t.astype(BF16), SSM_GROUP_IN)

    def powers(base, n):
        tr, ti = base
        while tr.shape[0] < n:
            qr, qi = tr[-1:], ti[-1:]
            tr, ti = (jnp.concatenate([tr, tr * qr - ti * qi], axis=0),
                      jnp.concatenate([ti, tr * qi + ti * qr], axis=0))
        return tr, ti

    assert seg_len & (seg_len - 1) == 0
    a_row = (abr.reshape(1, SSM_LANES), abi.reshape(1, SSM_LANES))
    ones = jnp.ones((SUBLANES, 1), F32)
    a_b = jnp.stack([ones * a_row[0], ones * a_row[1]])
    pw_r, pw_i = powers(a_row, seg_len)
    sp_r, sp_i = powers((pw_r[-1:], pw_i[-1:]), SUBLANES)
    row = jnp.arange(SUBLANES)[:, None]
    seg = []
    for s in SCAN_STEPS:
        seg += [jnp.where(row >= s, sp_r[s - 1:s], 0.0), jnp.where(row >= s, sp_i[s - 1:s], 0.0)]
    seg += [sp_r, sp_i]
    pw = jnp.stack([jnp.repeat(pw_r, SUBLANES, axis=0), jnp.repeat(pw_i, SUBLANES, axis=0)])
    return {"wb_re": in_map(bbr), "wb_im": in_map(bbi), "wc_re": out_map(c_re), "wc_im": out_map(c_im),
            "a": a_b, "seg": jnp.stack(seg), "pw": pw, "d": d_skip.reshape(1, SSM_WIDTH),
            "glu_w": glu_w.astype(BF16), "glu_b": glu_b.reshape(1, SSM_WIDTH)}


def _attend(q, k, v):
    s = lax.dot_general(q, k, (((1,), (1,)), ((), ())), preferred_element_type=F32)
    s = s * (XA_HEAD_DIM ** -0.5)
    p = jnp.exp(s - jnp.max(s, axis=-1, keepdims=True))
    p = p / jnp.sum(p, axis=-1, keepdims=True)
    return _dot(p.astype(BF16), v)


def _xa_prompt_kernel(q_ref, k_ref, v_ref, o_ref):
    for h in range(XA_HEADS):
        cols = slice(h * XA_HEAD_DIM, (h + 1) * XA_HEAD_DIM)
        o_ref[:, cols] = _attend(q_ref[:, cols], k_ref[:, cols], v_ref[:, cols]).astype(BF16)


def _xa_prompt(q, k, v, *, batch, seq, rows):
    nl = seq // rows
    return pl.pallas_call(
        _xa_prompt_kernel,
        grid=(batch, nl),
        in_specs=[pl.BlockSpec((rows, XA_WIDTH), lambda b, l: (b * nl + l, 0)),
                  pl.BlockSpec((N_MEM, XA_WIDTH), lambda b, l: (b, 0)),
                  pl.BlockSpec((N_MEM, XA_WIDTH), lambda b, l: (b, 0))],
        out_specs=pl.BlockSpec((rows, XA_WIDTH), lambda b, l: (b * nl + l, 0)),
        out_shape=jax.ShapeDtypeStruct((batch * seq, XA_WIDTH), BF16),
        compiler_params=_params("parallel", "arbitrary"),
        name="xattn_prompt",
    )(q, k, v)


KV_ROWS = N_MEM * XA_HALVES * XA_HEADS


def _kv_rows(cache):
    b = cache.shape[0]
    c = cache.reshape(b, N_MEM, XA_HEADS, XA_HALVES, LANES).transpose(0, 1, 3, 2, 4)
    return c.reshape(b, KV_ROWS, LANES)


def _kv_from_rows(rows):
    b = rows.shape[0]
    c = rows.reshape(b, N_MEM, XA_HALVES, XA_HEADS, LANES).transpose(0, 1, 3, 2, 4)
    return c.reshape(b, N_MEM, XA_HEADS, XA_HEAD_DIM)


def _memory_kv_kernel(x_ref, g_ref, wk_ref, wv_ref, k_ref, v_ref, kr_ref, vr_ref, xn_ref, *, batch):
    j = pl.program_id(0)

    @pl.when(j == 0)
    def _():
        _norm_to_scratch(x_ref, g_ref, xn_ref)

    for w_ref, o_ref, rows_ref in ((wk_ref, k_ref, kr_ref), (wv_ref, v_ref, vr_ref)):
        y = _dot(xn_ref[...], w_ref[...].astype(BF16))
        o_ref[...] = y.astype(BF16)
        for h in range(XA_HEADS):
            @pl.when(j == h)
            def _():
                for b in range(batch):
                    for t in range(XA_HALVES):
                        rows_ref[b, pl.ds(t * XA_HEADS + h, N_MEM, stride=XA_HALVES * XA_HEADS), :] = (
                            y[b * N_MEM:(b + 1) * N_MEM, t * LANES:(t + 1) * LANES])


def _memory_kv(mem, g, wk, wv):
    batch, n_mem, d = mem.shape
    rows_spec = pl.BlockSpec((batch, KV_ROWS, LANES), lambda j: (0, 0, 0))
    head_spec = pl.BlockSpec((batch * n_mem, XA_HEAD_DIM), lambda j: (0, j))
    w_spec = pl.BlockSpec((d, XA_HEAD_DIM), lambda j: (0, j))
    return pl.pallas_call(
        functools.partial(_memory_kv_kernel, batch=batch),
        grid=(XA_HEADS,),
        in_specs=[_resident((batch * n_mem, d)), _resident((1, d)), w_spec, w_spec],
        out_specs=[head_spec, head_spec, rows_spec, rows_spec],
        out_shape=[jax.ShapeDtypeStruct((batch * n_mem, XA_WIDTH), BF16)] * 2
        + [jax.ShapeDtypeStruct((batch, KV_ROWS, LANES), F32)] * 2,
        scratch_shapes=[pltpu.VMEM((batch * n_mem, d), BF16)],
        compiler_params=_params("arbitrary"),
        name="memory_kv",
    )(mem.reshape(batch * n_mem, d), g.reshape(1, d), wk, wv)


def _xa_sample_kernel(q_ref, k_ref, v_ref, o_ref, *, bb, seq):
    hl = XA_HEADS * seq
    n_rows = N_MEM * XA_HALVES * XA_HEADS
    col = lax.broadcasted_iota(jnp.int32, (hl, n_rows), 1)
    row = lax.broadcasted_iota(jnp.int32, (hl, n_rows), 0)
    slot = col & (XA_HALVES * XA_HEADS - 1)
    head = row // seq
    first_half = slot == head
    second_half = slot == head + XA_HEADS
    q_all = q_ref[...].astype(F32)
    for b in range(bb):
        rows = slice(b * seq, (b + 1) * seq)
        q = q_all[rows, :]
        qm = jnp.concatenate([q[:, h * XA_HEAD_DIM + t * LANES:h * XA_HEAD_DIM + (t + 1) * LANES]
                              for t in range(XA_HALVES) for h in range(XA_HEADS)], axis=0)
        g = lax.dot_general(qm.astype(BF16), k_ref[b].astype(BF16), (((1,), (1,)), ((), ())),
                            preferred_element_type=F32)
        g0 = jnp.where(first_half, g[:hl], 0.0)
        g1 = jnp.where(second_half, g[hl:], 0.0)
        s = (g0 + pltpu.roll(g1, n_rows - XA_HEADS, 1)) * (XA_HEAD_DIM ** -0.5)
        s = jnp.where(first_half, s, -jnp.inf)
        p = jnp.exp(s - jnp.max(s, axis=-1, keepdims=True))
        p = p / jnp.sum(p, axis=-1, keepdims=True)
        pp = jnp.concatenate([p, pltpu.roll(p, XA_HEADS, 1)], axis=0).astype(BF16)
        o = _dot(pp, v_ref[b].astype(BF16))
        for t in range(XA_HALVES):
            for h in range(XA_HEADS):
                r0 = t * hl + h * seq
                c0 = h * XA_HEAD_DIM + t * LANES
                o_ref[rows, c0:c0 + LANES] = o[r0:r0 + seq].astype(BF16)


def _xa_sample(z, mem_k, mem_v, *, batch, seq, bb):
    n_rows = N_MEM * XA_HALVES * XA_HEADS
    return pl.pallas_call(
        functools.partial(_xa_sample_kernel, bb=bb, seq=seq),
        grid=(batch // bb,),
        in_specs=[pl.BlockSpec((bb * seq, XA_WIDTH), lambda i: (i, 0)),
                  pl.BlockSpec((bb, n_rows, LANES), lambda i: (i, 0, 0)),
                  pl.BlockSpec((bb, n_rows, LANES), lambda i: (i, 0, 0))],
        out_specs=pl.BlockSpec((bb * seq, XA_WIDTH), lambda i: (i, 0)),
        out_shape=jax.ShapeDtypeStruct((batch * seq, XA_WIDTH), BF16),
        compiler_params=_params("parallel"),
        name="xattn_sample",
    )(z, mem_k, mem_v)


def _merge_kernel(ap_ref, as_ref, ax_ref, g0_ref, g1_ref, g2_ref, x_ref, wp_ref, ws_ref, wx_ref, wo_ref, gn_ref,
                  o_ref, on_ref):
    merged = (g0_ref[...] * _dot(ap_ref[...], wp_ref[...])
              + g1_ref[...] * _dot(as_ref[...], ws_ref[...])
              + g2_ref[...] * _dot(ax_ref[...], wx_ref[...]))
    x = x_ref[...] + _dot(merged.astype(BF16), wo_ref[...])
    o_ref[...] = x
    on_ref[...] = ((x * _rms_scale(x)) * gn_ref[...]).astype(BF16)


def _merge(a_pool, a_ssm, a_xa, gates, x, wp, ws, wx, wo, g_next, *, tm):
    m, d = x.shape
    act = pl.BlockSpec((tm, a_pool.shape[1]), lambda i: (i, 0))
    row = pl.BlockSpec((tm, d), lambda i: (i, 0))
    return pl.pallas_call(
        _merge_kernel,
        grid=(m // tm,),
        in_specs=[act, act, act,
                  pl.BlockSpec((tm, d), lambda i: (i, 0)),
                  pl.BlockSpec((tm, d), lambda i: (i, 1)),
                  pl.BlockSpec((tm, d), lambda i: (i, 2)),
                  row,
                  _resident(wp.shape), _resident(ws.shape), _resident(wx.shape), _resident(wo.shape),
                  _resident((1, d))],
        out_specs=[row, row],
        out_shape=[jax.ShapeDtypeStruct((m, d), F32), jax.ShapeDtypeStruct((m, d), BF16)],
        compiler_params=_params("parallel"),
        name="merge",
    )(a_pool, a_ssm, a_xa, gates, gates, gates, x, wp, ws, wx, wo, g_next.reshape(1, d))


def _mlp_kernel(x_ref, xn_ref, w1_ref, w2_ref, gf_ref, o_ref, *wb_refs, emit_w):
    if emit_w:
        w1b_ref, w2b_ref = wb_refs
        w1b_ref[...] = w1_ref[...].astype(BF16)
        w2b_ref[...] = w2_ref[...].astype(BF16)
    else:
        w1b_ref, w2b_ref = w1_ref, w2_ref
    f = pl.program_id(1)
    last = pl.num_programs(1) - 1

    def plus_update(acc):
        h = jnp.square(jnp.maximum(_dot(xn_ref[...], w1b_ref[...]), 0.0))
        return acc + _dot(h.astype(BF16), w2b_ref[...])

    @pl.when(f == 0)
    def _():
        o_ref[...] = plus_update(x_ref[...])

    @pl.when(jnp.logical_and(f > 0, f < last))
    def _():
        o_ref[...] = plus_update(o_ref[...])

    @pl.when(f == last)
    def _():
        y = plus_update(o_ref[...])
        o_ref[...] = (y * _rms_scale(y)) * gf_ref[...]


def _mlp(x, xn, w1, w2, gf, *, tm, tf, name):
    m, d = x.shape
    dff = w1.shape[1]
    assert dff // tf >= 2
    emit_w = w1.dtype != BF16
    assert not emit_w or m == tm, "the bf16 weight copies are written once, by a single row tile"
    w1_spec = pl.BlockSpec((d, tf), lambda i, f: (0, f))
    w2_spec = pl.BlockSpec((tf, d), lambda i, f: (f, 0))
    row_spec = pl.BlockSpec((tm, d), lambda i, f: (i, 0), pipeline_mode=pl.Buffered(1) if m == tm else None)
    out_specs = [row_spec]
    out_shape = [jax.ShapeDtypeStruct((m, d), F32)]
    if emit_w:
        out_specs += [w1_spec, w2_spec]
        out_shape += [jax.ShapeDtypeStruct(w1.shape, BF16), jax.ShapeDtypeStruct(w2.shape, BF16)]
    return pl.pallas_call(
        functools.partial(_mlp_kernel, emit_w=emit_w),
        grid=(m // tm, dff // tf),
        in_specs=[row_spec, row_spec, w1_spec, w2_spec, pl.BlockSpec((1, d), lambda i, f: (0, 0))],
        out_specs=out_specs,
        out_shape=out_shape,
        compiler_params=_params("parallel", "arbitrary"),
        name=name,
    )(x, xn, w1, w2, gf.reshape(1, d))


def _layer_weights(l, norm1_g, w_in, b_gate, pool_w, pool_scale, pool_proj, ssm_A_re, ssm_A_im, ssm_log_dt,
                   ssm_B_re, ssm_B_im, ssm_C_re, ssm_C_im, ssm_D, ssm_glu_w, ssm_glu_b, ssm_proj,
                   mem_norm_g, xa_wk, xa_wv, xa_wo, w_out, norm2_g, mlp_w1, mlp_w2):
    return {
        "norm1_g": norm1_g[l], "w_in": w_in[l], "b_gate": b_gate[l].reshape(-1),
        "pool_w": pool_w[l].astype(BF16), "pool_scale": pool_scale[l].reshape(1, POOL_WIDTH),
        "pool_proj": pool_proj[l].astype(BF16),
        "ssm": _ssm_weights(ssm_A_re[l], ssm_A_im[l], ssm_log_dt[l], ssm_B_re[l], ssm_B_im[l],
                            ssm_C_re[l], ssm_C_im[l], ssm_D[l], ssm_glu_w[l], ssm_glu_b[l],
                            seg_len=SSM_PROMPT_ROWS // SUBLANES),
        "ssm_proj": ssm_proj[l].astype(BF16),
        "mem_norm_g": mem_norm_g[l],
        "xa_wk": xa_wk[l], "xa_wv": xa_wv[l],
        "xa_wo": xa_wo[l].astype(BF16), "w_out": w_out[l].astype(BF16),
        "norm2_g": norm2_g[l], "mlp_w1": mlp_w1[l], "mlp_w2": mlp_w2[l],
    }


def _merge_branches(x, gates, a_pool, a_ssm, a_xa, w):
    return _merge(a_pool, a_ssm, a_xa, gates, x, w["pool_proj"], w["ssm_proj"], w["xa_wo"], w["w_out"],
                  w["norm2_g"], tm=MERGE_TM)


def kernel(x_prompt, x_sample, state_pool, state_ssm_re, state_ssm_im, cache_mem_k, cache_mem_v, mem_prompt,
           norm1_g, w_in, b_gate, pool_w, pool_scale, pool_proj, ssm_A_re, ssm_A_im, ssm_log_dt, ssm_B_re,
           ssm_B_im, ssm_C_re, ssm_C_im, ssm_D, ssm_glu_w, ssm_glu_b, ssm_proj, mem_norm_g, xa_wk, xa_wv,
           xa_wo, w_out, norm2_g, mlp_w1, mlp_w2, final_norm_g):
    depth = norm1_g.shape[0]
    assert depth == 1, "the final norm is fused into the single layer's MLP kernel"
    bp, lp, d = x_prompt.shape
    bs, ls, _ = x_sample.shape
    n_prev_sample = min(POOL_BUF, PAST_LEN)
    w = _layer_weights(0, norm1_g, w_in, b_gate, pool_w, pool_scale, pool_proj, ssm_A_re, ssm_A_im, ssm_log_dt,
                       ssm_B_re, ssm_B_im, ssm_C_re, ssm_C_im, ssm_D, ssm_glu_w, ssm_glu_b, ssm_proj,
                       mem_norm_g, xa_wk, xa_wv, xa_wo, w_out, norm2_g, mlp_w1, mlp_w2)

    xs = x_sample.reshape(bs * ls, d)
    assert xs.shape[0] == IN_TM == MLP_TM
    us, qs, gs, w_in_b = _in_proj(xs, w["norm1_g"], w["w_in"], w["b_gate"], tm=IN_TM, tn=IN_SAMPLE_TN,
                                  name="in_proj_sample")
    a_pool, pool_s = _pool_sample(us, state_pool, w["pool_w"], w["pool_scale"], batch=bs, seq=ls,
                                  bb=POOL_SAMPLE_SEQS, n_prev=n_prev_sample)
    a_ssm, re_s, im_s = _ssm_sample(us, state_ssm_re[0].reshape(bs, SSM_LANES),
                                    state_ssm_im[0].reshape(bs, SSM_LANES), w["ssm"], batch=bs, seq=ls,
                                    bb=SSM_SAMPLE_SEQS)
    a_xa = _xa_sample(qs, _kv_rows(cache_mem_k[0]), _kv_rows(cache_mem_v[0]), batch=bs, seq=ls, bb=XA_SAMPLE_SEQS)
    xs2, xs2n = _merge_branches(xs, gs, a_pool, a_ssm, a_xa, w)
    y_sample, w1_b, w2_b = _mlp(xs2, xs2n, w["mlp_w1"], w["mlp_w2"], final_norm_g, tm=MLP_TM,
                                tf=MLP_TF, name="mlp_sample")
    y_sample = y_sample.reshape(bs, ls, d)

    xp = x_prompt.reshape(bp * lp, d)
    kp, vp, mk_rows, mv_rows = _memory_kv(mem_prompt, w["mem_norm_g"], w["xa_wk"], w["xa_wv"])
    up, qp, gp = _in_proj(xp, w["norm1_g"], w_in_b, w["b_gate"], tm=IN_TM, tn=IN_TN, name="in_proj_prompt")
    a_pool, pool_p = _pool_prompt(up, w["pool_w"], w["pool_scale"], batch=bp, seq=lp, rows=POOL_PROMPT_ROWS)
    a_ssm, re_p, im_p = _ssm_prompt(up, w["ssm"], batch=bp, seq=lp, rows=SSM_PROMPT_ROWS)
    a_xa = _xa_prompt(qp, kp, vp, batch=bp, seq=lp, rows=XA_PROMPT_ROWS)
    xp2, xp2n = _merge_branches(xp, gp, a_pool, a_ssm, a_xa, w)
    (y_prompt,) = _mlp(xp2, xp2n, w1_b, w2_b, final_norm_g, tm=MLP_TM, tf=MLP_TF, name="mlp_prompt")
    y_prompt = y_prompt.reshape(bp, lp, d)

    state_shape = (1, -1, SSM_GROUPS, SSM_STATE)
    return (y_prompt, y_sample,
            pool_p, re_p.reshape(state_shape), im_p.reshape(state_shape),
            _kv_from_rows(mk_rows)[None], _kv_from_rows(mv_rows)[None],
            pool_s, re_s.reshape(state_shape), im_s.reshape(state_shape))
```

```python
import functools
import math

import jax
import jax.numpy as jnp
from jax import lax
from jax.experimental import pallas as pl
from jax.experimental.pallas import tpu as pltpu

F32 = jnp.float32
BF16 = jnp.bfloat16

D_MODEL = 2048
POOL_WINDOWS = (2, 4, 8, 16)
POOL_GROUP_W = 256
POOL_WIDTH = 1024
POOL_BUF = 15
SSM_WIDTH = 1024
SSM_GROUPS = 64
SSM_GROUP_IN = 16
SSM_STATE = 64
SSM_LANES = SSM_GROUPS * SSM_STATE
SSM_CHUNKS = 4
SSM_CHUNK_GROUPS = SSM_GROUPS // SSM_CHUNKS
SSM_CHUNK_IN = SSM_WIDTH // SSM_CHUNKS
SSM_CHUNK_LANES = SSM_LANES // SSM_CHUNKS
XA_HEADS = 4
XA_HEAD_DIM = 256
XA_WIDTH = 1024
N_MEM = 256
N_BRANCH = 3
D_FF = 4 * D_MODEL
EPS = 1e-6
PAST_LEN = 16384

SUBLANES = 8
LANES = 128
XA_HALVES = XA_HEAD_DIM // LANES
POOL_HIST = 2 * SUBLANES
SCAN_STEPS = (1, 2, 4)
SCAN_LANE_BLOCK = 1024
SSM_PROMPT_ROWS = 256

IN_TM = 1024
IN_TN = 1024
IN_SAMPLE_TN = 512
MERGE_TM = 256
MLP_TM = 1024
MLP_TF = 512
POOL_PROMPT_ROWS = 512
XA_PROMPT_ROWS = 512
POOL_SAMPLE_SEQS = 16
SSM_SAMPLE_SEQS = 32
XA_SAMPLE_SEQS = 4
VMEM_LIMIT = 56 * 1024 * 1024


def _params(*sem):
    return pltpu.CompilerParams(dimension_semantics=sem, vmem_limit_bytes=VMEM_LIMIT)


def _resident(shape):
    nd = len(shape)
    return pl.BlockSpec(shape, lambda *_: (0,) * nd, pipeline_mode=pl.Buffered(1))


def _rms_scale(x):
    return lax.rsqrt(jnp.mean(x * x, axis=-1, keepdims=True) + EPS)


def _dot(a, b):
    return jnp.dot(a, b, preferred_element_type=F32)


def _sigmoid(x):
    return 0.5 * jnp.tanh(0.5 * x) + 0.5


def _norm_to_scratch(x_ref, g_ref, xn_ref):
    x = x_ref[...]
    xn_ref[...] = ((x * _rms_scale(x)) * g_ref[...]).astype(BF16)


def _in_proj_kernel(x_ref, g_ref, w_ref, b_ref, u_ref, q_ref, gate_ref, *rest, u_blocks, gate0, emit_w):
    if emit_w:
        wb_ref, xn_ref = rest
    else:
        (xn_ref,) = rest
    j = pl.program_id(1)

    @pl.when(j == 0)
    def _():
        _norm_to_scratch(x_ref, g_ref, xn_ref)

    if emit_w:
        wb_ref[...] = w_ref[...].astype(BF16)

    def w_block():
        return wb_ref[...] if emit_w else w_ref[...]

    @pl.when(j < u_blocks)
    def _():
        u_ref[...] = _dot(xn_ref[...], w_block())

    @pl.when(jnp.logical_and(j >= u_blocks, j < gate0))
    def _():
        q_ref[...] = _dot(xn_ref[...], w_block()).astype(BF16)

    @pl.when(j >= gate0)
    def _():
        gate_ref[...] = _sigmoid(_dot(xn_ref[...], w_block()) + b_ref[...]).astype(BF16)


def _in_proj(x, g, w_in, b_gate, *, tm, tn, name):
    m, d = x.shape
    n = w_in.shape[1]
    emit_w = w_in.dtype != BF16
    assert not emit_w or m == tm, "the bf16 weight copy is written once, by a single row tile"
    u_blocks = (POOL_WIDTH + SSM_WIDTH) // tn
    q_blocks = XA_WIDTH // tn
    gate0 = u_blocks + q_blocks
    n_gate = n // tn - gate0
    out_specs = [pl.BlockSpec((tm, tn), lambda i, j: (i, jnp.minimum(j, u_blocks - 1))),
                 pl.BlockSpec((tm, tn), lambda i, j: (i, jnp.clip(j - u_blocks, 0, q_blocks - 1))),
                 pl.BlockSpec((tm, tn), lambda i, j: (i, jnp.maximum(j - gate0, 0)))]
    out_shape = [jax.ShapeDtypeStruct((m, u_blocks * tn), F32),
                 jax.ShapeDtypeStruct((m, q_blocks * tn), BF16),
                 jax.ShapeDtypeStruct((m, n_gate * tn), BF16)]
    if emit_w:
        out_specs.append(pl.BlockSpec((d, tn), lambda i, j: (0, j)))
        out_shape.append(jax.ShapeDtypeStruct((d, n), BF16))
    return pl.pallas_call(
        functools.partial(_in_proj_kernel, u_blocks=u_blocks, gate0=gate0, emit_w=emit_w),
        grid=(m // tm, n // tn),
        in_specs=[pl.BlockSpec((tm, d), lambda i, j: (i, 0)),
                  pl.BlockSpec((1, d), lambda i, j: (0, 0)),
                  pl.BlockSpec((d, tn), lambda i, j: (0, j)),
                  pl.BlockSpec((1, tn), lambda i, j: (0, jnp.maximum(j - gate0, 0)))],
        out_specs=out_specs,
        out_shape=out_shape,
        scratch_shapes=[pltpu.VMEM((tm, d), BF16)],
        compiler_params=_params("parallel", "arbitrary"),
        name=name,
    )(x, g.reshape(1, d), w_in, b_gate.reshape(1, n_gate * tn))


def _window_means(e_ref, row0, rows, first_pos):
    outs = []
    for k, wdw in enumerate(POOL_WINDOWS):
        lo = k * POOL_GROUP_W
        cur = e_ref[row0:row0 + rows, lo:lo + POOL_GROUP_W]
        s = cur
        for d in range(1, wdw):
            s = s + e_ref[row0 - d:row0 - d + rows, lo:lo + POOL_GROUP_W]
        pos = lax.broadcasted_iota(jnp.int32, (rows, POOL_GROUP_W), 0) + first_pos
        cnt = jnp.minimum(pos, wdw).astype(F32)
        outs.append(s / cnt - cur)
    return outs


def _pool_mix(pooled, k, pw_ref, ps_ref):
    lo = k * POOL_GROUP_W
    mixed = _dot(pooled.astype(BF16), pw_ref[k])
    return (mixed * ps_ref[:, lo:lo + POOL_GROUP_W]).astype(BF16)


def _pool_prompt_kernel(u_ref, pw_ref, ps_ref, o_ref, st_ref, e_ref, *, rows):
    l = pl.program_id(1)

    @pl.when(l == 0)
    def _():
        e_ref[0:POOL_HIST, :] = jnp.zeros((POOL_HIST, POOL_WIDTH), F32)

    @pl.when(l > 0)
    def _():
        e_ref[0:POOL_HIST, :] = e_ref[rows:rows + POOL_HIST, :]

    e_ref[POOL_HIST:POOL_HIST + rows, :] = u_ref[...]
    pooled = _window_means(e_ref, POOL_HIST, rows, l * rows + 1)
    for k, p in enumerate(pooled):
        o_ref[:, k * POOL_GROUP_W:(k + 1) * POOL_GROUP_W] = _pool_mix(p, k, pw_ref, ps_ref)

    @pl.when(l == pl.num_programs(1) - 1)
    def _():
        st_ref[0] = e_ref[POOL_HIST + rows - POOL_BUF:POOL_HIST + rows, :]


def _pool_prompt(z, pool_w, pool_scale, *, batch, seq, rows):
    nl = seq // rows
    return pl.pallas_call(
        functools.partial(_pool_prompt_kernel, rows=rows),
        grid=(batch, nl),
        in_specs=[pl.BlockSpec((rows, POOL_WIDTH), lambda b, l: (b * nl + l, 0)),
                  _resident(pool_w.shape), _resident(pool_scale.shape)],
        out_specs=[pl.BlockSpec((rows, POOL_WIDTH), lambda b, l: (b * nl + l, 0)),
                   pl.BlockSpec((None, 1, POOL_BUF, POOL_WIDTH), lambda b, l: (0, b, 0, 0))],
        out_shape=[jax.ShapeDtypeStruct((batch * seq, POOL_WIDTH), BF16),
                   jax.ShapeDtypeStruct((1, batch, POOL_BUF, POOL_WIDTH), F32)],
        scratch_shapes=[pltpu.VMEM((rows + POOL_HIST, POOL_WIDTH), F32)],
        compiler_params=_params("parallel", "arbitrary"),
        name="pool_prompt",
    )(z, pool_w, pool_scale)


def _pool_sample_kernel(u_ref, prev_ref, pw_ref, ps_ref, o_ref, st_ref, e_ref, p_ref, *, bb, seq, n_prev):
    span = POOL_HIST + seq
    for b in range(bb):
        top = b * span + POOL_HIST
        e_ref[top - POOL_BUF:top, :] = prev_ref[b]
        e_ref[top:top + seq, :] = u_ref[b * seq:(b + 1) * seq, :]
    for b in range(bb):
        top = b * span + POOL_HIST
        st_ref[b] = e_ref[top + seq - POOL_BUF:top + seq, :]
        pooled = _window_means(e_ref, top, seq, n_prev + 1)
        for k, p in enumerate(pooled):
            p_ref[b * seq:(b + 1) * seq, k * POOL_GROUP_W:(k + 1) * POOL_GROUP_W] = p
    for k in range(len(POOL_WINDOWS)):
        lo = k * POOL_GROUP_W
        o_ref[:, lo:lo + POOL_GROUP_W] = _pool_mix(p_ref[:, lo:lo + POOL_GROUP_W], k, pw_ref, ps_ref)


def _pool_sample(z, prev, pool_w, pool_scale, *, batch, seq, bb, n_prev):
    return pl.pallas_call(
        functools.partial(_pool_sample_kernel, bb=bb, seq=seq, n_prev=n_prev),
        grid=(batch // bb,),
        in_specs=[pl.BlockSpec((bb * seq, POOL_WIDTH), lambda i: (i, 0)),
                  pl.BlockSpec((None, bb, POOL_BUF, POOL_WIDTH), lambda i: (0, i, 0, 0)),
                  _resident(pool_w.shape), _resident(pool_scale.shape)],
        out_specs=[pl.BlockSpec((bb * seq, POOL_WIDTH), lambda i: (i, 0)),
                   pl.BlockSpec((None, bb, POOL_BUF, POOL_WIDTH), lambda i: (0, i, 0, 0))],
        out_shape=[jax.ShapeDtypeStruct((batch * seq, POOL_WIDTH), BF16),
                   jax.ShapeDtypeStruct((1, batch, POOL_BUF, POOL_WIDTH), F32)],
        scratch_shapes=[pltpu.VMEM((bb * (POOL_HIST + seq), POOL_WIDTH), F32),
                        pltpu.VMEM((bb * seq, POOL_WIDTH), F32)],
        compiler_params=_params("parallel"),
        name="pool_sample",
    )(z, prev, pool_w, pool_scale)


def _regroup_matrices(rows, seg_len):
    dst = jnp.arange(rows)
    g, j, s = dst // (SUBLANES * seg_len), (dst // SUBLANES) % seg_len, dst % SUBLANES
    src = (g * SUBLANES + s) * seg_len + j
    p = (src[:, None] == jnp.arange(rows)[None, :]).astype(BF16)
    return p, p.T


def _regroup_rows(u_ref, p_ref, up_ref):
    u = u_ref[...]
    u1 = u.astype(BF16)
    r1 = u - u1.astype(F32)
    u2 = r1.astype(BF16)
    u3 = (r1 - u2.astype(F32)).astype(BF16)
    p = p_ref[...]
    up_ref[...] = (_dot(p, u1) + _dot(p, u2)) + _dot(p, u3)


def _chunk_lanes(c):
    return slice(c * SSM_CHUNK_LANES, (c + 1) * SSM_CHUNK_LANES)


def _chunk_cols(c):
    return slice(c * SSM_CHUNK_IN, (c + 1) * SSM_CHUNK_IN)


def _ssm_input_map(c, up_ref, wbr_ref, wbi_ref, hr_ref, hi_ref):
    uc = up_ref[:, _chunk_cols(c)].astype(BF16)
    hr_ref[:, _chunk_lanes(c)] = _dot(uc, wbr_ref[c])
    hi_ref[:, _chunk_lanes(c)] = _dot(uc, wbi_ref[c])


def _ssm_chunks(input_map, scan, output_map):
    input_map(0)
    for c in range(SSM_CHUNKS):
        if c + 1 < SSM_CHUNKS:
            input_map(c + 1)
        scan(c)
        if c > 0:
            output_map(c - 1)
    output_map(SSM_CHUNKS - 1)


def _scan_to_bf16(hr_ref, hi_ref, hbr_ref, hbi_ref, lanes, row0, steps, a_r, a_i, h_r, h_i):
    for jj in range(steps // 2):
        out_r, out_i = [], []
        for j in (2 * jj, 2 * jj + 1):
            rs = slice(row0 + j * SUBLANES, row0 + (j + 1) * SUBLANES)
            h_r, h_i = _cmul_add(a_r, a_i, h_r, h_i, hr_ref[rs, lanes], hi_ref[rs, lanes])
            out_r.append(h_r)
            out_i.append(h_i)
        rp = slice(row0 + jj * 2 * SUBLANES, row0 + (jj + 1) * 2 * SUBLANES)
        hbr_ref[rp, lanes] = jnp.concatenate(out_r, axis=0).astype(BF16)
        hbi_ref[rp, lanes] = jnp.concatenate(out_i, axis=0).astype(BF16)
    return h_r, h_i


def _cmul_add(ar, ai, hr, hi, xr, xi):
    return xr + (ar * hr - ai * hi), xi + (ar * hi + ai * hr)


def _scan_rows(xr, xi, coef, car_r, car_i):
    for n, s in enumerate(SCAN_STEPS):
        xr, xi = _cmul_add(coef[2 * n], coef[2 * n + 1], pltpu.roll(xr, s, 0), pltpu.roll(xi, s, 0), xr, xi)
    return _cmul_add(coef[2 * len(SCAN_STEPS)], coef[2 * len(SCAN_STEPS) + 1], car_r, car_i, xr, xi)


def _last_row(x):
    return jnp.broadcast_to(x[SUBLANES - 1:SUBLANES, :], x.shape)


def _ssm_output_map(c, up_ref, hbr_ref, hbi_ref, wcr_ref, wci_ref, d_ref, zg_ref):
    y = _dot(hbr_ref[:, _chunk_lanes(c)], wcr_ref[c]) - _dot(hbi_ref[:, _chunk_lanes(c)], wci_ref[c])
    zg_ref[:, _chunk_cols(c)] = jax.nn.gelu(y + d_ref[:, _chunk_cols(c)] * up_ref[:, _chunk_cols(c)])


def _ssm_glu(pt_ref, gw_ref, gb_ref, zg_ref, o_ref):
    zg = zg_ref[...]
    gate = _sigmoid(_dot(zg.astype(BF16), gw_ref[...]) + gb_ref[...])
    o_ref[...] = _dot(pt_ref[...], (zg * gate).astype(BF16)).astype(BF16)


def _ssm_prompt_kernel(u_ref, p_ref, pt_ref, wbr_ref, wbi_ref, wcr_ref, wci_ref, a_ref, seg_ref, d_ref, gw_ref,
                       gb_ref, o_ref, sr_ref, si_ref, up_ref, hr_ref, hi_ref, hbr_ref, hbi_ref, zg_ref, car_ref,
                       *, rows):
    seg_len = rows // SUBLANES

    @pl.when(pl.program_id(1) == 0)
    def _():
        car_ref[...] = jnp.zeros(car_ref.shape, F32)

    _regroup_rows(u_ref, p_ref, up_ref)

    def scan(c):
        lanes = _chunk_lanes(c)
        a_r, a_i = a_ref[0, :, lanes], a_ref[1, :, lanes]

        h_r = h_i = jnp.zeros((SUBLANES, SSM_CHUNK_LANES), F32)
        for j in range(seg_len):
            rs = slice(j * SUBLANES, (j + 1) * SUBLANES)
            h_r, h_i = _cmul_add(a_r, a_i, h_r, h_i, hr_ref[rs, lanes], hi_ref[rs, lanes])

        car_r, car_i = car_ref[0, :, lanes], car_ref[1, :, lanes]
        seg = tuple(seg_ref[n, :, lanes] for n in range(2 * len(SCAN_STEPS) + 2))
        end_r, end_i = _scan_rows(h_r, h_i, seg, car_r, car_i)
        car_ref[0, :, lanes] = _last_row(end_r)
        car_ref[1, :, lanes] = _last_row(end_i)
        sr_ref[0, :, lanes] = end_r[SUBLANES - 1:SUBLANES, :]
        si_ref[0, :, lanes] = end_i[SUBLANES - 1:SUBLANES, :]
        first = lax.broadcasted_iota(jnp.int32, (SUBLANES, SSM_CHUNK_LANES), 0) == 0
        in_r = jnp.where(first, car_r, pltpu.roll(end_r, 1, 0))
        in_i = jnp.where(first, car_i, pltpu.roll(end_i, 1, 0))

        _scan_to_bf16(hr_ref, hi_ref, hbr_ref, hbi_ref, lanes, 0, seg_len, a_r, a_i, in_r, in_i)

    _ssm_chunks(lambda c: _ssm_input_map(c, up_ref, wbr_ref, wbi_ref, hr_ref, hi_ref), scan,
                lambda c: _ssm_output_map(c, up_ref, hbr_ref, hbi_ref, wcr_ref, wci_ref, d_ref, zg_ref))
    _ssm_glu(pt_ref, gw_ref, gb_ref, zg_ref, o_ref)


def _ssm_common_specs(w, rows):
    return [_resident((rows, rows)), _resident((rows, rows)), _resident(w["wb_re"].shape),
            _resident(w["wb_im"].shape), _resident(w["wc_re"].shape), _resident(w["wc_im"].shape),
            _resident(w["a"].shape)]


def _ssm_common_args(w, rows, seg_len):
    return [*_regroup_matrices(rows, seg_len), w["wb_re"], w["wb_im"], w["wc_re"], w["wc_im"], w["a"]]


def _ssm_tail_specs(w):
    return [_resident(w["d"].shape), _resident(w["glu_w"].shape), _resident(w["glu_b"].shape)]


def _ssm_scratch(rows):
    return [pltpu.VMEM((rows, SSM_WIDTH), F32),
            pltpu.VMEM((rows, SSM_LANES), F32), pltpu.VMEM((rows, SSM_LANES), F32),
            pltpu.VMEM((rows, SSM_LANES), BF16), pltpu.VMEM((rows, SSM_LANES), BF16),
            pltpu.VMEM((rows, SSM_WIDTH), F32)]


def _ssm_prompt(z, w, *, batch, seq, rows):
    nl = seq // rows
    return pl.pallas_call(
        functools.partial(_ssm_prompt_kernel, rows=rows),
        grid=(batch, nl),
        in_specs=([pl.BlockSpec((rows, SSM_WIDTH), lambda b, l: (b * nl + l, 1))] + _ssm_common_specs(w, rows)
                  + [_resident(w["seg"].shape)] + _ssm_tail_specs(w)),
        out_specs=[pl.BlockSpec((rows, SSM_WIDTH), lambda b, l: (b * nl + l, 0)),
                   pl.BlockSpec((1, 1, SSM_LANES), lambda b, l: (b, 0, 0)),
                   pl.BlockSpec((1, 1, SSM_LANES), lambda b, l: (b, 0, 0))],
        out_shape=[jax.ShapeDtypeStruct((batch * seq, SSM_WIDTH), BF16),
                   jax.ShapeDtypeStruct((batch, 1, SSM_LANES), F32),
                   jax.ShapeDtypeStruct((batch, 1, SSM_LANES), F32)],
        scratch_shapes=_ssm_scratch(rows) + [pltpu.VMEM((2, SUBLANES, SSM_LANES), F32)],
        compiler_params=_params("parallel", "arbitrary"),
        name="ssm_prompt",
    )(z, *_ssm_common_args(w, rows, rows // SUBLANES), w["seg"], w["d"], w["glu_w"], w["glu_b"])


def _ssm_sample_kernel(u_ref, h0r_ref, h0i_ref, p_ref, pt_ref, wbr_ref, wbi_ref, wcr_ref, wci_ref, a_ref, d_ref,
                       gw_ref, gb_ref, o_ref, sr_ref, si_ref, up_ref, hr_ref, hi_ref, hbr_ref, hbi_ref, zg_ref,
                       *, bb, seq):
    _regroup_rows(u_ref, p_ref, up_ref)

    def scan(c):
        lanes = _chunk_lanes(c)
        a_r, a_i = a_ref[0, :, lanes], a_ref[1, :, lanes]
        for g in range(bb // SUBLANES):
            seqs = slice(g * SUBLANES, (g + 1) * SUBLANES)
            h_r, h_i = _scan_to_bf16(hr_ref, hi_ref, hbr_ref, hbi_ref, lanes, g * seq * SUBLANES, seq, a_r, a_i,
                                     h0r_ref[seqs, lanes], h0i_ref[seqs, lanes])
            sr_ref[seqs, lanes] = h_r
            si_ref[seqs, lanes] = h_i

    _ssm_chunks(lambda c: _ssm_input_map(c, up_ref, wbr_ref, wbi_ref, hr_ref, hi_ref), scan,
                lambda c: _ssm_output_map(c, up_ref, hbr_ref, hbi_ref, wcr_ref, wci_ref, d_ref, zg_ref))
    _ssm_glu(pt_ref, gw_ref, gb_ref, zg_ref, o_ref)


def _ssm_sample(z, h_re, h_im, w, *, batch, seq, bb):
    assert seq % 2 == 0 and bb % SUBLANES == 0
    rows = bb * seq
    return pl.pallas_call(
        functools.partial(_ssm_sample_kernel, bb=bb, seq=seq),
        grid=(batch // bb,),
        in_specs=([pl.BlockSpec((rows, SSM_WIDTH), lambda i: (i, 1)),
                   pl.BlockSpec((bb, SSM_LANES), lambda i: (i, 0)),
                   pl.BlockSpec((bb, SSM_LANES), lambda i: (i, 0))] + _ssm_common_specs(w, rows)
                  + _ssm_tail_specs(w)),
        out_specs=[pl.BlockSpec((rows, SSM_WIDTH), lambda i: (i, 0)),
                   pl.BlockSpec((bb, SSM_LANES), lambda i: (i, 0)),
                   pl.BlockSpec((bb, SSM_LANES), lambda i: (i, 0))],
        out_shape=[jax.ShapeDtypeStruct((batch * seq, SSM_WIDTH), BF16),
                   jax.ShapeDtypeStruct((batch, SSM_LANES), F32),
                   jax.ShapeDtypeStruct((batch, SSM_LANES), F32)],
        scratch_shapes=_ssm_scratch(rows),
        compiler_params=_params("parallel"),
        name="ssm_sample",
    )(z, h_re, h_im, *_ssm_common_args(w, rows, seq), w["d"], w["glu_w"], w["glu_b"])


def _ssm_weights(a_re, a_im, log_dt, b_re, b_im, c_re, c_im, d_skip, glu_w, glu_b, *, seg_len):
    dt = jnp.exp(log_dt)[:, None]
    mag = jnp.exp(dt * a_re)
    abr, abi = mag * jnp.cos(dt * a_im), mag * jnp.sin(dt * a_im)
    den = a_re * a_re + a_im * a_im
    xr, xi = abr - 1.0, abi
    fr = (xr * a_re + xi * a_im) / den
    fi = (xi * a_re - xr * a_im) / den
    bbr = fr[..., None] * b_re - fi[..., None] * b_im
    bbi = fr[..., None] * b_im + fi[..., None] * b_re

    def block_diag(t, width):
        rows = t.shape[1]
        per_group = rows // SSM_CHUNK_GROUPS
        cols = SSM_CHUNK_GROUPS * width
        tile = (jnp.arange(width)[:, None] == jnp.arange(cols)[None, :] % width).astype(BF16)
        tiled = jnp.einsum("crk,kn->crn", t, tile, preferred_element_type=F32)
        same = (jnp.arange(rows)[:, None] // per_group) == (jnp.arange(cols)[None, :] // width)
        return jnp.where(same[None], tiled, 0.0).astype(BF16)

    def in_map(bb):
        t = bb.transpose(0, 2, 1).reshape(SSM_CHUNKS, SSM_CHUNK_IN, SSM_STATE)
        return block_diag(t.astype(BF16), SSM_STATE)

    def out_map(cc):
        t = cc.transpose(0, 2, 1).reshape(SSM_CHUNKS, SSM_CHUNK_LANES, SSM_GROUP_IN)
        return block_diag(t.astype(BF16), SSM_GROUP_IN)

    def powers(base, n):
        tr, ti = base
        while tr.shape[0] < n:
            qr, qi = tr[-1:], ti[-1:]
            tr, ti = (jnp.concatenate([tr, tr * qr - ti * qi], axis=0),
                      jnp.concatenate([ti, tr * qi + ti * qr], axis=0))
        return tr, ti

    assert seg_len & (seg_len - 1) == 0
    a_row = (abr.reshape(1, SSM_LANES), abi.reshape(1, SSM_LANES))
    ones = jnp.ones((SUBLANES, 1), F32)
    a_b = jnp.stack([ones * a_row[0], ones * a_row[1]])
    pw_r, pw_i = powers(a_row, seg_len)
    sp_r, sp_i = powers((pw_r[-1:], pw_i[-1:]), SUBLANES)
    row = jnp.arange(SUBLANES)[:, None]
    seg = []
    for s in SCAN_STEPS:
        seg += [jnp.where(row >= s, sp_r[s - 1:s], 0.0), jnp.where(row >= s, sp_i[s - 1:s], 0.0)]
    seg += [sp_r, sp_i]
    return {"wb_re": in_map(bbr), "wb_im": in_map(bbi), "wc_re": out_map(c_re), "wc_im": out_map(c_im),
            "a": a_b, "seg": jnp.stack(seg), "d": d_skip.reshape(1, SSM_WIDTH),
            "glu_w": glu_w.astype(BF16), "glu_b": glu_b.reshape(1, SSM_WIDTH)}


def _attend(q, k, v):
    s = lax.dot_general(q, k, (((1,), (1,)), ((), ())), preferred_element_type=F32)
    s = s * (XA_HEAD_DIM ** -0.5)
    p = jnp.exp(s - jnp.max(s, axis=-1, keepdims=True))
    p = p / jnp.sum(p, axis=-1, keepdims=True)
    return _dot(p.astype(BF16), v)


def _xa_prompt_kernel(q_ref, k_ref, v_ref, o_ref):
    for h in range(XA_HEADS):
        cols = slice(h * XA_HEAD_DIM, (h + 1) * XA_HEAD_DIM)
        o_ref[:, cols] = _attend(q_ref[:, cols], k_ref[:, cols], v_ref[:, cols]).astype(BF16)


def _xa_prompt(q, k, v, *, batch, seq, rows):
    nl = seq // rows
    return pl.pallas_call(
        _xa_prompt_kernel,
        grid=(batch, nl),
        in_specs=[pl.BlockSpec((rows, XA_WIDTH), lambda b, l: (b * nl + l, 0)),
                  pl.BlockSpec((N_MEM, XA_WIDTH), lambda b, l: (b, 0)),
                  pl.BlockSpec((N_MEM, XA_WIDTH), lambda b, l: (b, 0))],
        out_specs=pl.BlockSpec((rows, XA_WIDTH), lambda b, l: (b * nl + l, 0)),
        out_shape=jax.ShapeDtypeStruct((batch * seq, XA_WIDTH), BF16),
        compiler_params=_params("parallel", "arbitrary"),
        name="xattn_prompt",
    )(q, k, v)


KV_ROWS = N_MEM * XA_HALVES * XA_HEADS


def _kv_rows(cache):
    b = cache.shape[0]
    c = cache.reshape(b, N_MEM, XA_HEADS, XA_HALVES, LANES).transpose(0, 1, 3, 2, 4)
    return c.reshape(b, KV_ROWS, LANES)


def _kv_from_rows(rows):
    b = rows.shape[0]
    c = rows.reshape(b, N_MEM, XA_HALVES, XA_HEADS, LANES).transpose(0, 1, 3, 2, 4)
    return c.reshape(b, N_MEM, XA_HEADS, XA_HEAD_DIM)


def _memory_kv_kernel(x_ref, g_ref, wk_ref, wv_ref, k_ref, v_ref, kr_ref, vr_ref, xn_ref, *, batch):
    j = pl.program_id(0)

    @pl.when(j == 0)
    def _():
        _norm_to_scratch(x_ref, g_ref, xn_ref)

    for w_ref, o_ref, rows_ref in ((wk_ref, k_ref, kr_ref), (wv_ref, v_ref, vr_ref)):
        y = _dot(xn_ref[...], w_ref[...].astype(BF16))
        o_ref[...] = y.astype(BF16)
        for h in range(XA_HEADS):
            @pl.when(j == h)
            def _():
                for b in range(batch):
                    for t in range(XA_HALVES):
                        rows_ref[b, pl.ds(t * XA_HEADS + h, N_MEM, stride=XA_HALVES * XA_HEADS), :] = (
                            y[b * N_MEM:(b + 1) * N_MEM, t * LANES:(t + 1) * LANES])


def _memory_kv(mem, g, wk, wv):
    batch, n_mem, d = mem.shape
    rows_spec = pl.BlockSpec((batch, KV_ROWS, LANES), lambda j: (0, 0, 0))
    head_spec = pl.BlockSpec((batch * n_mem, XA_HEAD_DIM), lambda j: (0, j))
    w_spec = pl.BlockSpec((d, XA_HEAD_DIM), lambda j: (0, j))
    return pl.pallas_call(
        functools.partial(_memory_kv_kernel, batch=batch),
        grid=(XA_HEADS,),
        in_specs=[_resident((batch * n_mem, d)), _resident((1, d)), w_spec, w_spec],
        out_specs=[head_spec, head_spec, rows_spec, rows_spec],
        out_shape=[jax.ShapeDtypeStruct((batch * n_mem, XA_WIDTH), BF16)] * 2
        + [jax.ShapeDtypeStruct((batch, KV_ROWS, LANES), F32)] * 2,
        scratch_shapes=[pltpu.VMEM((batch * n_mem, d), BF16)],
        compiler_params=_params("arbitrary"),
        name="memory_kv",
    )(mem.reshape(batch * n_mem, d), g.reshape(1, d), wk, wv)


def _xa_sample_kernel(q_ref, k_ref, v_ref, o_ref, *, bb, seq):
    hl = XA_HEADS * seq
    n_rows = N_MEM * XA_HALVES * XA_HEADS
    col = lax.broadcasted_iota(jnp.int32, (hl, n_rows), 1)
    row = lax.broadcasted_iota(jnp.int32, (hl, n_rows), 0)
    slot = col & (XA_HALVES * XA_HEADS - 1)
    head = row // seq
    first_half = slot == head
    second_half = slot == head + XA_HEADS
    q_all = q_ref[...].astype(F32)
    for b in range(bb):
        rows = slice(b * seq, (b + 1) * seq)
        q = q_all[rows, :]
        qm = jnp.concatenate([q[:, h * XA_HEAD_DIM + t * LANES:h * XA_HEAD_DIM + (t + 1) * LANES]
                              for t in range(XA_HALVES) for h in range(XA_HEADS)], axis=0)
        g = lax.dot_general(qm.astype(BF16), k_ref[b].astype(BF16), (((1,), (1,)), ((), ())),
                            preferred_element_type=F32)
        g0 = jnp.where(first_half, g[:hl], 0.0)
        g1 = jnp.where(second_half, g[hl:], 0.0)
        s = (g0 + pltpu.roll(g1, n_rows - XA_HEADS, 1)) * (XA_HEAD_DIM ** -0.5)
        s = jnp.where(first_half, s, -jnp.inf)
        p = jnp.exp(s - jnp.max(s, axis=-1, keepdims=True))
        p = p / jnp.sum(p, axis=-1, keepdims=True)
        pp = jnp.concatenate([p, pltpu.roll(p, XA_HEADS, 1)], axis=0).astype(BF16)
        o = _dot(pp, v_ref[b].astype(BF16))
        for t in range(XA_HALVES):
            for h in range(XA_HEADS):
                r0 = t * hl + h * seq
                c0 = h * XA_HEAD_DIM + t * LANES
                o_ref[rows, c0:c0 + LANES] = o[r0:r0 + seq].astype(BF16)


def _xa_sample(z, mem_k, mem_v, *, batch, seq, bb):
    n_rows = N_MEM * XA_HALVES * XA_HEADS
    return pl.pallas_call(
        functools.partial(_xa_sample_kernel, bb=bb, seq=seq),
        grid=(batch // bb,),
        in_specs=[pl.BlockSpec((bb * seq, XA_WIDTH), lambda i: (i, 0)),
                  pl.BlockSpec((bb, n_rows, LANES), lambda i: (i, 0, 0)),
                  pl.BlockSpec((bb, n_rows, LANES), lambda i: (i, 0, 0))],
        out_specs=pl.BlockSpec((bb * seq, XA_WIDTH), lambda i: (i, 0)),
        out_shape=jax.ShapeDtypeStruct((batch * seq, XA_WIDTH), BF16),
        compiler_params=_params("parallel"),
        name="xattn_sample",
    )(z, mem_k, mem_v)


def _merge_kernel(ap_ref, as_ref, ax_ref, g0_ref, g1_ref, g2_ref, x_ref, wp_ref, ws_ref, wx_ref, wo_ref, gn_ref,
                  o_ref, on_ref):
    merged = (g0_ref[...] * _dot(ap_ref[...], wp_ref[...])
              + g1_ref[...] * _dot(as_ref[...], ws_ref[...])
              + g2_ref[...] * _dot(ax_ref[...], wx_ref[...]))
    x = x_ref[...] + _dot(merged.astype(BF16), wo_ref[...])
    o_ref[...] = x
    on_ref[...] = ((x * _rms_scale(x)) * gn_ref[...]).astype(BF16)


def _merge(a_pool, a_ssm, a_xa, gates, x, wp, ws, wx, wo, g_next, *, tm):
    m, d = x.shape
    act = pl.BlockSpec((tm, a_pool.shape[1]), lambda i: (i, 0))
    row = pl.BlockSpec((tm, d), lambda i: (i, 0))
    return pl.pallas_call(
        _merge_kernel,
        grid=(m // tm,),
        in_specs=[act, act, act,
                  pl.BlockSpec((tm, d), lambda i: (i, 0)),
                  pl.BlockSpec((tm, d), lambda i: (i, 1)),
                  pl.BlockSpec((tm, d), lambda i: (i, 2)),
                  row,
                  _resident(wp.shape), _resident(ws.shape), _resident(wx.shape), _resident(wo.shape),
                  _resident((1, d))],
        out_specs=[row, row],
        out_shape=[jax.ShapeDtypeStruct((m, d), F32), jax.ShapeDtypeStruct((m, d), BF16)],
        compiler_params=_params("parallel"),
        name="merge",
    )(a_pool, a_ssm, a_xa, gates, gates, gates, x, wp, ws, wx, wo, g_next.reshape(1, d))


def _mlp_kernel(x_ref, xn_ref, w1_ref, w2_ref, gf_ref, o_ref, *wb_refs, emit_w):
    if emit_w:
        w1b_ref, w2b_ref = wb_refs
        w1b_ref[...] = w1_ref[...].astype(BF16)
        w2b_ref[...] = w2_ref[...].astype(BF16)
    else:
        w1b_ref, w2b_ref = w1_ref, w2_ref
    f = pl.program_id(1)
    last = pl.num_programs(1) - 1

    def plus_update(acc):
        h = jnp.square(jnp.maximum(_dot(xn_ref[...], w1b_ref[...]), 0.0))
        return acc + _dot(h.astype(BF16), w2b_ref[...])

    @pl.when(f == 0)
    def _():
        o_ref[...] = plus_update(x_ref[...])

    @pl.when(jnp.logical_and(f > 0, f < last))
    def _():
        o_ref[...] = plus_update(o_ref[...])

    @pl.when(f == last)
    def _():
        y = plus_update(o_ref[...])
        o_ref[...] = (y * _rms_scale(y)) * gf_ref[...]


def _mlp(x, xn, w1, w2, gf, *, tm, tf, name):
    m, d = x.shape
    dff = w1.shape[1]
    assert dff // tf >= 2
    emit_w = w1.dtype != BF16
    assert not emit_w or m == tm, "the bf16 weight copies are written once, by a single row tile"
    w1_spec = pl.BlockSpec((d, tf), lambda i, f: (0, f))
    w2_spec = pl.BlockSpec((tf, d), lambda i, f: (f, 0))
    row_spec = pl.BlockSpec((tm, d), lambda i, f: (i, 0), pipeline_mode=pl.Buffered(1) if m == tm else None)
    out_specs = [row_spec]
    out_shape = [jax.ShapeDtypeStruct((m, d), F32)]
    if emit_w:
        out_specs += [w1_spec, w2_spec]
        out_shape += [jax.ShapeDtypeStruct(w1.shape, BF16), jax.ShapeDtypeStruct(w2.shape, BF16)]
    return pl.pallas_call(
        functools.partial(_mlp_kernel, emit_w=emit_w),
        grid=(m // tm, dff // tf),
        in_specs=[row_spec, row_spec, w1_spec, w2_spec, pl.BlockSpec((1, d), lambda i, f: (0, 0))],
        out_specs=out_specs,
        out_shape=out_shape,
        compiler_params=_params("parallel", "arbitrary"),
        name=name,
    )(x, xn, w1, w2, gf.reshape(1, d))


def _layer_weights(l, norm1_g, w_in, b_gate, pool_w, pool_scale, pool_proj, ssm_A_re, ssm_A_im, ssm_log_dt,
                   ssm_B_re, ssm_B_im, ssm_C_re, ssm_C_im, ssm_D, ssm_glu_w, ssm_glu_b, ssm_proj,
                   mem_norm_g, xa_wk, xa_wv, xa_wo, w_out, norm2_g, mlp_w1, mlp_w2):
    return {
        "norm1_g": norm1_g[l], "w_in": w_in[l], "b_gate": b_gate[l].reshape(-1),
        "pool_w": pool_w[l].astype(BF16), "pool_scale": pool_scale[l].reshape(1, POOL_WIDTH),
        "pool_proj": pool_proj[l].astype(BF16),
        "ssm": _ssm_weights(ssm_A_re[l], ssm_A_im[l], ssm_log_dt[l], ssm_B_re[l], ssm_B_im[l],
                            ssm_C_re[l], ssm_C_im[l], ssm_D[l], ssm_glu_w[l], ssm_glu_b[l],
                            seg_len=SSM_PROMPT_ROWS // SUBLANES),
        "ssm_proj": ssm_proj[l].astype(BF16),
        "mem_norm_g": mem_norm_g[l],
        "xa_wk": xa_wk[l], "xa_wv": xa_wv[l],
        "xa_wo": xa_wo[l].astype(BF16), "w_out": w_out[l].astype(BF16),
        "norm2_g": norm2_g[l], "mlp_w1": mlp_w1[l], "mlp_w2": mlp_w2[l],
    }


def _merge_branches(x, gates, a_pool, a_ssm, a_xa, w):
    return _merge(a_pool, a_ssm, a_xa, gates, x, w["pool_proj"], w["ssm_proj"], w["xa_wo"], w["w_out"],
                  w["norm2_g"], tm=MERGE_TM)


def kernel(x_prompt, x_sample, state_pool, state_ssm_re, state_ssm_im, cache_mem_k, cache_mem_v, mem_prompt,
           norm1_g, w_in, b_gate, pool_w, pool_scale, pool_proj, ssm_A_re, ssm_A_im, ssm_log_dt, ssm_B_re,
           ssm_B_im, ssm_C_re, ssm_C_im, ssm_D, ssm_glu_w, ssm_glu_b, ssm_proj, mem_norm_g, xa_wk, xa_wv,
           xa_wo, w_out, norm2_g, mlp_w1, mlp_w2, final_norm_g):
    depth = norm1_g.shape[0]
    assert depth == 1, "the final norm is fused into the single layer's MLP kernel"
    bp, lp, d = x_prompt.shape
    bs, ls, _ = x_sample.shape
    n_prev_sample = min(POOL_BUF, PAST_LEN)
    w = _layer_weights(0, norm1_g, w_in, b_gate, pool_w, pool_scale, pool_proj, ssm_A_re, ssm_A_im, ssm_log_dt,
                       ssm_B_re, ssm_B_im, ssm_C_re, ssm_C_im, ssm_D, ssm_glu_w, ssm_glu_b, ssm_proj,
                       mem_norm_g, xa_wk, xa_wv, xa_wo, w_out, norm2_g, mlp_w1, mlp_w2)

    xs = x_sample.reshape(bs * ls, d)
    assert xs.shape[0] == IN_TM == MLP_TM
    us, qs, gs, w_in_b = _in_proj(xs, w["norm1_g"], w["w_in"], w["b_gate"], tm=IN_TM, tn=IN_SAMPLE_TN,
                                  name="in_proj_sample")
    a_pool, pool_s = _pool_sample(us, state_pool, w["pool_w"], w["pool_scale"], batch=bs, seq=ls,
                                  bb=POOL_SAMPLE_SEQS, n_prev=n_prev_sample)
    a_ssm, re_s, im_s = _ssm_sample(us, state_ssm_re[0].reshape(bs, SSM_LANES),
                                    state_ssm_im[0].reshape(bs, SSM_LANES), w["ssm"], batch=bs, seq=ls,
                                    bb=SSM_SAMPLE_SEQS)
    a_xa = _xa_sample(qs, _kv_rows(cache_mem_k[0]), _kv_rows(cache_mem_v[0]), batch=bs, seq=ls, bb=XA_SAMPLE_SEQS)
    xs2, xs2n = _merge_branches(xs, gs, a_pool, a_ssm, a_xa, w)
    y_sample, w1_b, w2_b = _mlp(xs2, xs2n, w["mlp_w1"], w["mlp_w2"], final_norm_g, tm=MLP_TM,
                                tf=MLP_TF, name="mlp_sample")
    y_sample = y_sample.reshape(bs, ls, d)

    xp = x_prompt.reshape(bp * lp, d)
    kp, vp, mk_rows, mv_rows = _memory_kv(mem_prompt, w["mem_norm_g"], w["xa_wk"], w["xa_wv"])
    up, qp, gp = _in_proj(xp, w["norm1_g"], w_in_b, w["b_gate"], tm=IN_TM, tn=IN_TN, name="in_proj_prompt")
    a_pool, pool_p = _pool_prompt(up, w["pool_w"], w["pool_scale"], batch=bp, seq=lp, rows=POOL_PROMPT_ROWS)
    a_ssm, re_p, im_p = _ssm_prompt(up, w["ssm"], batch=bp, seq=lp, rows=SSM_PROMPT_ROWS)
    a_xa = _xa_prompt(qp, kp, vp, batch=bp, seq=lp, rows=XA_PROMPT_ROWS)
    xp2, xp2n = _merge_branches(xp, gp, a_pool, a_ssm, a_xa, w)
    (y_prompt,) = _mlp(xp2, xp2n, w1_b, w2_b, final_norm_g, tm=MLP_TM, tf=MLP_TF, name="mlp_prompt")
    y_prompt = y_prompt.reshape(bp, lp, d)

    state_shape = (1, -1, SSM_GROUPS, SSM_STATE)
    return (y_prompt, y_sample,
            pool_p, re_p.reshape(state_shape), im_p.reshape(state_shape),
            _kv_from_rows(mk_rows)[None], _kv_from_rows(mv_rows)[None],
            pool_s, re_s.reshape(state_shape), im_s.reshape(state_shape))
```
